```python
import jax, jax.numpy as jnp
from jax import lax
import numpy as np

D_MODEL = 4096
BATCH = 4
SEQ = 2048
DEPTH = 1
DEC_BATCH = 128
DEC_SEQ = 4
PAST_LEN = 16384
PAGE_SIZE = 128

D_SC = D_MODEL // 2
SC_CONV_W = 3
SSD_HEADDIM = 64
D_SSD = D_MODEL
SSD_HEADS = D_SSD // SSD_HEADDIM
SSD_GROUPS = 8
SSD_D_STATE = 128
SSD_CONV_W = 4
SSD_CHUNK = 128
D_XBC = D_SSD + 2 * SSD_GROUPS * SSD_D_STATE
D_MIX = D_SC + D_SSD
D_IN = 3 * D_SC + D_SSD + D_XBC + SSD_HEADS
D_FF = ((8 * D_MODEL // 3 + 255) // 256) * 256
FFN_CONV_W = 3
N_MOD = 6
EPS = 1e-6

kernel_name = 'hybrid_shortconv_ssd_convffn_adaln_step'


def rmsnorm(x, g):
    xf = x.astype(jnp.float32)
    r = lax.rsqrt(jnp.mean(xf * xf, axis=-1, keepdims=True) + EPS)
    return (xf * r).astype(x.dtype) * g


def causal_dwconv(u, buf, w, b=None):
    K = w.shape[0]
    L = u.shape[1]
    cat = jnp.concatenate([buf.astype(u.dtype), u], axis=1)
    out = cat[:, 0:L] * w[0]
    for k in range(1, K):
        out = out + cat[:, k:k + L] * w[k]
    if b is not None:
        out = out + b
    return out, cat[:, L:]


def ssd_scan(x, dt, A, Bm, Cm, init_state):
    b, L, H, P = x.shape
    G, N = Bm.shape[2], Bm.shape[3]
    R = H // G
    Q = min(SSD_CHUNK, L)
    nc = -(-L // Q)
    pad = nc * Q - L

    def chunk(t):
        t = t.astype(jnp.float32)
        t = jnp.pad(t, [(0, 0), (0, pad)] + [(0, 0)] * (t.ndim - 2))
        return t.reshape((b, nc, Q) + t.shape[2:])

    xc = chunk(x).reshape(b, nc, Q, G, R, P)
    dtc = chunk(dt).reshape(b, nc, Q, G, R)
    Bc = chunk(Bm)
    Cc = chunk(Cm)
    xdt = xc * dtc[..., None]
    ac = jnp.moveaxis(jnp.cumsum(dtc * A.reshape(G, R), axis=2), 2, -1)
    seg = ac[..., :, None] - ac[..., None, :]
    causal = jnp.tril(jnp.ones((Q, Q), dtype=bool))
    decay_in = jnp.exp(jnp.where(causal, seg, -jnp.inf))
    cb = jnp.einsum('bcign,bcjgn->bcgij', Cc, Bc)
    y_diag = jnp.einsum('bcgij,bcgrij,bcjgrp->bcigrp', cb, decay_in, xdt)
    decay_to_end = jnp.exp(ac[..., -1:] - ac)
    states = jnp.einsum('bcjgn,bcgrj,bcjgrp->bcgrpn', Bc, decay_to_end, xdt)
    chunk_decay = jnp.exp(ac[..., -1])

    def step(carry, inp):
        st, dec = inp
        return carry * dec[..., None, None] + st, carry

    s0 = init_state.astype(jnp.float32).reshape(b, G, R, P, N)
    final, prev = lax.scan(step, s0, (jnp.moveaxis(states, 1, 0), jnp.moveaxis(chunk_decay, 1, 0)))
    prev = jnp.moveaxis(prev, 0, 1)
    y_off = jnp.einsum('bcign,bcgri,bcgrpn->bcigrp', Cc, jnp.exp(ac), prev)
    y = (y_diag + y_off).reshape(b, nc * Q, H, P)[:, :L]
    return y.astype(x.dtype), final.reshape(b, H, P, N).astype(x.dtype)


def layer(x, c, sc_buf, ssd_buf, ssm0, ffn_buf,
          w_ada, b_ada, g_norm1, w_in, w_sc_conv, w_ssd_conv, b_ssd_conv,
          dt_bias, a_log, d_skip, g_ssd_norm, w_out, g_norm2, w_up, w_ffn_conv,
          b_ffn_conv, w_down):
    b, L, _ = x.shape
    mod = (jax.nn.silu(c) @ w_ada + b_ada).reshape(b, N_MOD, D_MODEL)[:, :, None, :]
    shift1, scale1, gate1, shift2, scale2, gate2 = [mod[:, i] for i in range(N_MOD)]

    h = rmsnorm(x, g_norm1) * (1.0 + scale1) + shift1
    proj = h @ w_in
    cuts = [int(v) for v in np.cumsum([D_SC, D_SC, D_SC, D_SSD, D_XBC])]
    sc_b, sc_c, sc_x, z, xbc, dt_raw = jnp.split(proj, cuts, axis=-1)
    u = sc_c * sc_x
    uc, new_sc_buf = causal_dwconv(u, sc_buf, w_sc_conv)
    y_sc = sc_b * uc
    xbc_c, new_ssd_buf = causal_dwconv(xbc, ssd_buf, w_ssd_conv, b_ssd_conv)
    xbc_c = jax.nn.silu(xbc_c)
    xs, Bm, Cm = jnp.split(xbc_c, [D_SSD, D_SSD + SSD_GROUPS * SSD_D_STATE], axis=-1)
    xs = xs.reshape(b, L, SSD_HEADS, SSD_HEADDIM)
    Bm = Bm.reshape(b, L, SSD_GROUPS, SSD_D_STATE)
    Cm = Cm.reshape(b, L, SSD_GROUPS, SSD_D_STATE)
    dt = jax.nn.softplus(dt_raw.astype(jnp.float32) + dt_bias.astype(jnp.float32))
    A = -jnp.exp(a_log.astype(jnp.float32))
    y_ssd, new_ssm = ssd_scan(xs, dt, A, Bm, Cm, ssm0)
    y_ssd = (y_ssd + d_skip[:, None] * xs).reshape(b, L, D_SSD)
    y_ssd = rmsnorm(y_ssd * jax.nn.silu(z), g_ssd_norm)
    mix = jnp.concatenate([y_sc, y_ssd], axis=-1) @ w_out
    x = x + gate1 * mix

    h2 = rmsnorm(x, g_norm2) * (1.0 + scale2) + shift2
    up = h2 @ w_up
    upc, new_ffn_buf = causal_dwconv(up, ffn_buf, w_ffn_conv, b_ffn_conv)
    g, v = jnp.split(upc, 2, axis=-1)
    x = x + gate2 * ((jax.nn.silu(g) * v) @ w_down)
    return x, new_sc_buf, new_ssd_buf, new_ssm, new_ffn_buf


def setup_inputs(seed: int = 0) -> dict:
    key = jax.random.key(seed)
    ks = jax.random.split(key, 32)
    f = jnp.float32
    nrm = lambda k, s, sc: jax.random.normal(k, s, f) * sc
    dt0 = jnp.exp(jax.random.uniform(ks[20], (DEPTH, SSD_HEADS), f, np.log(1e-3), np.log(1e-1)))
    return {
        'x_prompt': nrm(ks[0], (BATCH, SEQ, D_MODEL), 1.0),
        'x_sample': nrm(ks[1], (DEC_BATCH, DEC_SEQ, D_MODEL), 1.0),
        'c_prompt': nrm(ks[2], (BATCH, D_MODEL), 1.0),
        'c_sample': nrm(ks[3], (DEC_BATCH, D_MODEL), 1.0),
        'state_sc_conv': nrm(ks[4], (DEPTH, DEC_BATCH, SC_CONV_W - 1, D_SC), 1.0),
        'state_ssd_conv': nrm(ks[5], (DEPTH, DEC_BATCH, SSD_CONV_W - 1, D_XBC), 1.0),
        'state_ssm': nrm(ks[6], (DEPTH, DEC_BATCH, SSD_HEADS, SSD_HEADDIM, SSD_D_STATE), 0.1),
        'state_ffn_conv': nrm(ks[7], (DEPTH, DEC_BATCH, FFN_CONV_W - 1, 2 * D_FF), 1.0),
        'w_ada': nrm(ks[8], (DEPTH, D_MODEL, N_MOD * D_MODEL), 0.5 * D_MODEL ** -0.5),
        'b_ada': nrm(ks[9], (DEPTH, N_MOD * D_MODEL), 0.02),
        'g_norm1': 1.0 + nrm(ks[10], (DEPTH, D_MODEL), 0.02),
        'w_in': nrm(ks[11], (DEPTH, D_MODEL, D_IN), D_MODEL ** -0.5),
        'w_sc_conv': nrm(ks[12], (DEPTH, SC_CONV_W, D_SC), SC_CONV_W ** -0.5),
        'w_ssd_conv': nrm(ks[13], (DEPTH, SSD_CONV_W, D_XBC), SSD_CONV_W ** -0.5),
        'b_ssd_conv': nrm(ks[14], (DEPTH, D_XBC), 0.02),
        'dt_bias': dt0 + jnp.log(-jnp.expm1(-dt0)),
        'a_log': jnp.log(jax.random.uniform(ks[15], (DEPTH, SSD_HEADS), f, 1.0, 16.0)),
        'd_skip': 1.0 + nrm(ks[16], (DEPTH, SSD_HEADS), 0.1),
        'g_ssd_norm': 1.0 + nrm(ks[17], (DEPTH, D_SSD), 0.02),
        'w_out': nrm(ks[18], (DEPTH, D_MIX, D_MODEL), D_MIX ** -0.5),
        'g_norm2': 1.0 + nrm(ks[19], (DEPTH, D_MODEL), 0.02),
        'w_up': nrm(ks[21], (DEPTH, D_MODEL, 2 * D_FF), D_MODEL ** -0.5),
        'w_ffn_conv': nrm(ks[22], (DEPTH, FFN_CONV_W, 2 * D_FF), FFN_CONV_W ** -0.5),
        'b_ffn_conv': nrm(ks[23], (DEPTH, 2 * D_FF), 0.02),
        'w_down': nrm(ks[24], (DEPTH, D_FF, D_MODEL), D_FF ** -0.5),
        'g_final': 1.0 + nrm(ks[25], (D_MODEL,), 0.02),
    }


def reference(x_prompt, x_sample, c_prompt, c_sample, state_sc_conv, state_ssd_conv,
              state_ssm, state_ffn_conv, w_ada, b_ada, g_norm1, w_in, w_sc_conv,
              w_ssd_conv, b_ssd_conv, dt_bias, a_log, d_skip, g_ssd_norm, w_out,
              g_norm2, w_up, w_ffn_conv, b_ffn_conv, w_down, g_final):
    bp = x_prompt.shape[0]
    dtp = x_prompt.dtype
    xp, xs = x_prompt, x_sample
    p_sc, p_ssd, p_ssm, p_ffn = [], [], [], []
    s_sc, s_ssd, s_ssm, s_ffn = [], [], [], []
    for l in range(DEPTH):
        params = (w_ada[l], b_ada[l], g_norm1[l], w_in[l], w_sc_conv[l], w_ssd_conv[l],
                  b_ssd_conv[l], dt_bias[l], a_log[l], d_skip[l], g_ssd_norm[l], w_out[l],
                  g_norm2[l], w_up[l], w_ffn_conv[l], b_ffn_conv[l], w_down[l])
        xp, a, b_, c_, d_ = layer(
            xp, c_prompt,
            jnp.zeros((bp, SC_CONV_W - 1, D_SC), dtp),
            jnp.zeros((bp, SSD_CONV_W - 1, D_XBC), dtp),
            jnp.zeros((bp, SSD_HEADS, SSD_HEADDIM, SSD_D_STATE), dtp),
            jnp.zeros((bp, FFN_CONV_W - 1, 2 * D_FF), dtp),
            *params)
        p_sc.append(a); p_ssd.append(b_); p_ssm.append(c_); p_ffn.append(d_)
        xs, a, b_, c_, d_ = layer(xs, c_sample, state_sc_conv[l], state_ssd_conv[l],
                                  state_ssm[l], state_ffn_conv[l], *params)
        s_sc.append(a); s_ssd.append(b_); s_ssm.append(c_); s_ffn.append(d_)
    y_prompt = rmsnorm(xp, g_final)
    y_sample = rmsnorm(xs, g_final)
    return (y_prompt, y_sample,
            jnp.stack(p_sc), jnp.stack(p_ssd), jnp.stack(p_ssm), jnp.stack(p_ffn),
            jnp.stack(s_sc), jnp.stack(s_ssd), jnp.stack(s_ssm), jnp.stack(s_ffn))
```

```python
import functools

import jax
import jax.numpy as jnp
from jax import lax
from jax.experimental import pallas as pl
from jax.experimental.pallas import tpu as pltpu

EPS = 1e-6
D_MODEL = 4096
D_SC = 2048
D_SSD = 4096
N_HEADS = 64
HEADDIM = 64
N_GROUPS = 8
HEADS_PER_GROUP = N_HEADS // N_GROUPS
D_STATE = 128
GROUP_W = HEADS_PER_GROUP * HEADDIM
D_BC = N_GROUPS * D_STATE
D_XBC = D_SSD + 2 * D_BC
D_MIX = D_SC + D_SSD
D_IN = 3 * D_SC + D_SSD + D_XBC + N_HEADS
D_FF = 11008
N_MOD = 6
CHUNK = 128
DEC_SEQ = 4

OFF_B, OFF_C, OFF_X, OFF_Z, OFF_XBC, OFF_DT = 0, 2048, 4096, 6144, 10240, 16384

VMEM_LIMIT_BYTES = 56 * 1024 * 1024

_NN = (((1,), (0,)), ((), ()))
_NT = (((1,), (1,)), ((), ()))
_TN = (((0,), (0,)), ((), ()))

f32 = jnp.float32
bf16 = jnp.bfloat16


def _dot(a, b, dims=_NN):
    return lax.dot_general(a, b, dims, preferred_element_type=f32)


def _split3(x):
    hi = x.astype(bf16)
    r1 = x - hi.astype(f32)
    mid = r1.astype(bf16)
    lo = (r1 - mid.astype(f32)).astype(bf16)
    return hi, mid, lo


def _dot3(x, m):
    hi, mid, lo = _split3(x)
    return _dot(hi, m) + _dot(mid, m) + _dot(lo, m)


def _dot3r(m, x):
    hi, mid, lo = _split3(x)
    return _dot(m, hi) + _dot(m, mid) + _dot(m, lo)


def _silu(x):
    return x * jax.nn.sigmoid(x)


def _softplus(x):
    return jnp.maximum(x, 0.0) + jnp.log1p(jnp.exp(-jnp.abs(x)))


def _params(*sem):
    return pltpu.CompilerParams(dimension_semantics=sem, vmem_limit_bytes=VMEM_LIMIT_BYTES)


def _ada_kernel(c_ref, w_ref, b_ref, o_ref):
    a = _silu(c_ref[...]).astype(bf16)
    o_ref[0] = _dot(a, w_ref[...]) + b_ref[...]


def _ada(c_all, w_ada, b_ada):
    rc = c_all.shape[0]
    tn = 512
    nb = D_MODEL // tn
    return pl.pallas_call(
        _ada_kernel,
        grid=(N_MOD * nb,),
        in_specs=[
            pl.BlockSpec((rc, D_MODEL), lambda j: (0, 0)),
            pl.BlockSpec((D_MODEL, tn), lambda j: (0, j)),
            pl.BlockSpec((1, tn), lambda j: (0, j)),
        ],
        out_specs=pl.BlockSpec((1, rc, tn), lambda j: (j // nb, 0, j % nb)),
        out_shape=jax.ShapeDtypeStruct((N_MOD, rc, D_MODEL), f32),
        compiler_params=_params("arbitrary"),
        name="ada",
    )(c_all, w_ada, b_ada)


def _mod_spec(mod, tm, tn, rows_per_seq, ndim_grid):
    if mod.ndim == 3:
        tiles_per_seq = rows_per_seq // tm
        if ndim_grid == 1:
            return pl.BlockSpec((None, 1, tn), lambda i: (i // tiles_per_seq, 0, 0))
        return pl.BlockSpec((None, 1, tn), lambda i, j: (i // tiles_per_seq, 0, j))
    if ndim_grid == 1:
        return pl.BlockSpec((tm, tn), lambda i: (i, 0))
    return pl.BlockSpec((tm, tn), lambda i, j: (i, j))


def _normmod_kernel(x_ref, g_ref, sc_ref, sh_ref, o_ref):
    x = x_ref[...]
    r = lax.rsqrt(jnp.mean(x * x, axis=-1, keepdims=True) + EPS)
    h = (x * r) * g_ref[...]
    o_ref[...] = (h * (1.0 + sc_ref[...]) + sh_ref[...]).astype(o_ref.dtype)


def _normmod(x, g, scale, shift, tm, rows_per_seq):
    rows = x.shape[0]
    return pl.pallas_call(
        _normmod_kernel,
        grid=(rows // tm,),
        in_specs=[
            pl.BlockSpec((tm, D_MODEL), lambda i: (i, 0)),
            pl.BlockSpec((1, D_MODEL), lambda i: (0, 0)),
            _mod_spec(scale, tm, D_MODEL, rows_per_seq, 1),
            _mod_spec(shift, tm, D_MODEL, rows_per_seq, 1),
        ],
        out_specs=pl.BlockSpec((tm, D_MODEL), lambda i: (i, 0)),
        out_shape=jax.ShapeDtypeStruct((rows, D_MODEL), bf16),
        compiler_params=_params("arbitrary"),
        name="normmod",
    )(x, g, scale, shift)


def _norm_kernel(x_ref, g_ref, o_ref):
    x = x_ref[...]
    r = lax.rsqrt(jnp.mean(x * x, axis=-1, keepdims=True) + EPS)
    o_ref[...] = (x * r) * g_ref[...]


def _final_norm(x, g, tm):
    rows = x.shape[0]
    return pl.pallas_call(
        _norm_kernel,
        grid=(rows // tm,),
        in_specs=[
            pl.BlockSpec((tm, D_MODEL), lambda i: (i, 0)),
            pl.BlockSpec((1, D_MODEL), lambda i: (0, 0)),
        ],
        out_specs=pl.BlockSpec((tm, D_MODEL), lambda i: (i, 0)),
        out_shape=jax.ShapeDtypeStruct((rows, D_MODEL), f32),
        compiler_params=_params("arbitrary"),
        name="final_norm",
    )(x, g)


def _mm_kernel(a_ref, w_ref, o_ref):
    o_ref[...] = _dot(a_ref[...], w_ref[...])


def _mm_res_kernel(a_ref, w_ref, x_ref, g_ref, o_ref):
    o_ref[...] = x_ref[...] + g_ref[...] * _dot(a_ref[...], w_ref[...])


def _matmul(a, w, tm, tn, name):
    m, k = a.shape
    n = w.shape[1]
    return pl.pallas_call(
        _mm_kernel,
        grid=(m // tm, pl.cdiv(n, tn)),
        in_specs=[
            pl.BlockSpec((tm, k), lambda i, j: (i, 0), pipeline_mode=pl.Buffered(1)),
            pl.BlockSpec((k, tn), lambda i, j: (0, j)),
        ],
        out_specs=pl.BlockSpec((tm, tn), lambda i, j: (i, j)),
        out_shape=jax.ShapeDtypeStruct((m, n), f32),
        compiler_params=_params("arbitrary", "arbitrary"),
        name=name,
    )(a, w)


def _matmul_res(a, w, x, gate, tm, tn, rows_per_seq, name):
    m, k = a.shape
    n = w.shape[1]
    return pl.pallas_call(
        _mm_res_kernel,
        grid=(m // tm, n // tn),
        in_specs=[
            pl.BlockSpec((tm, k), lambda i, j: (i, 0), pipeline_mode=pl.Buffered(1)),
            pl.BlockSpec((k, tn), lambda i, j: (0, j)),
            pl.BlockSpec((tm, tn), lambda i, j: (i, j)),
            _mod_spec(gate, tm, tn, rows_per_seq, 2),
        ],
        out_specs=pl.BlockSpec((tm, tn), lambda i, j: (i, j)),
        out_shape=jax.ShapeDtypeStruct((m, n), f32),
        compiler_params=_params("arbitrary", "arbitrary"),
        name=name,
    )(a, w, x, gate)


def _mixer_prompt_kernel(proj_ref, wsc_ref, wssd_ref, bssd_ref, dtb_ref, alog_ref, dexp_ref, gssd_ref,
                         mix_ref, scbuf_ref, ssdbuf_ref, state_ref,
                         uwin, xwin, xs_s, b_s, c_s, y_s, ac_s, dt_s, w_s, eac_s, act_s, cd_s):
    q = CHUNK
    c = pl.program_id(1)

    @pl.when(c == 0)
    def _init():
        uwin[0:8, :] = jnp.zeros((8, D_SC), f32)
        xwin[0:8, :] = jnp.zeros((8, D_XBC), f32)
        state_ref[...] = jnp.zeros(state_ref.shape, f32)

    u = proj_ref[:, OFF_C:OFF_C + D_SC] * proj_ref[:, OFF_X:OFF_X + D_SC]
    uwin[8:8 + q, :] = u
    uc = (wsc_ref[2:3, :] * u + wsc_ref[1:2, :] * uwin[7:7 + q, :] + wsc_ref[0:1, :] * uwin[6:6 + q, :])
    mix_ref[:, 0:D_SC] = (proj_ref[:, OFF_B:OFF_B + D_SC] * uc).astype(bf16)
    scbuf_ref[0] = uwin[q + 6:q + 8, :]
    uwin[0:8, :] = uwin[q:q + 8, :]

    xwin[8:8 + q, :] = proj_ref[:, OFF_XBC:OFF_XBC + D_XBC]

    def conv_slab(lo, width):
        acc = (wssd_ref[3:4, lo:lo + width] * xwin[8:8 + q, lo:lo + width]
               + wssd_ref[2:3, lo:lo + width] * xwin[7:7 + q, lo:lo + width]
               + wssd_ref[1:2, lo:lo + width] * xwin[6:6 + q, lo:lo + width]
               + wssd_ref[0:1, lo:lo + width] * xwin[5:5 + q, lo:lo + width]
               + bssd_ref[:, lo:lo + width])
        return _silu(acc)

    for g in range(N_GROUPS):
        xs_s[g] = conv_slab(g * GROUP_W, GROUP_W)
        b_s[g] = conv_slab(D_SSD + g * D_STATE, D_STATE)
        c_s[g] = conv_slab(D_SSD + D_BC + g * D_STATE, D_STATE)
    ssdbuf_ref[0] = xwin[q + 5:q + 8, :]
    xwin[0:8, :] = xwin[q:q + 8, :]

    dt = _softplus(proj_ref[:, OFF_DT:OFF_DT + N_HEADS] + dtb_ref[...])
    a = dt * (-jnp.exp(alog_ref[...]))
    row = lax.broadcasted_iota(jnp.int32, (q, q), 0)
    col = lax.broadcasted_iota(jnp.int32, (q, q), 1)
    tri = col <= row
    ac = _dot3r(jnp.where(tri, 1.0, 0.0).astype(bf16), a)
    ac_last = ac[q - 1:q, :]
    wv = dt * jnp.exp(ac_last - ac)
    eac = jnp.exp(ac)
    cd = jnp.exp(ac_last)
    ac_t = ac.T
    for g in range(N_GROUPS):
        hs = slice(g * HEADS_PER_GROUP, (g + 1) * HEADS_PER_GROUP)
        ac_s[g] = ac[:, hs]
        dt_s[g] = dt[:, hs]
        w_s[g] = wv[:, hs]
        eac_s[g] = eac[:, hs]
        cd_s[g] = cd[:, hs]
        act_s[g] = ac_t[hs, :]

    lane_lo = lax.broadcasted_iota(jnp.int32, (q, 2 * HEADDIM), 1) < HEADDIM

    def bcast_col(arr, r):
        return jnp.broadcast_to(arr[:, r:r + 1], (q, 2 * HEADDIM))

    def group_body(g, carry):
        bb = b_s[g].astype(bf16)
        cb_ = c_s[g].astype(bf16)
        cb = _dot(cb_, bb, _NT)
        h0 = pl.multiple_of(g * HEADS_PER_GROUP, HEADS_PER_GROUP)
        s_old = state_ref[0, pl.ds(h0, HEADS_PER_GROUP)].reshape(GROUP_W, D_STATE)
        yoff = _dot(cb_, s_old.astype(bf16), _NT)
        xg = xs_s[g]
        acg, dtg, wg, eg, atg, cdg = ac_s[g], dt_s[g], w_s[g], eac_s[g], act_s[g], cd_s[g]
        dsk = dexp_ref[g]
        xw_parts = []
        for pair in range(HEADS_PER_GROUP // 2):
            cs = slice(pair * 2 * HEADDIM, (pair + 1) * 2 * HEADDIM)
            r0, r1 = 2 * pair, 2 * pair + 1
            xp = xg[:, cs]
            xdt = xp * jnp.where(lane_lo, bcast_col(dtg, r0), bcast_col(dtg, r1))
            ydiag = jnp.zeros((q, 2 * HEADDIM), f32)
            for r, keep in ((r0, lane_lo), (r1, jnp.logical_not(lane_lo))):
                seg = bcast_col(acg, r) - jnp.broadcast_to(atg[r:r + 1, :], (q, q))
                decay = jnp.where(tri, jnp.exp(jnp.where(tri, seg, 0.0)), 0.0)
                m = (cb * decay).astype(bf16)
                ydiag = ydiag + _dot(m, jnp.where(keep, xdt, 0.0).astype(bf16))
            e_pair = jnp.where(lane_lo, bcast_col(eg, r0), bcast_col(eg, r1))
            y_s[g, :, cs] = ydiag + yoff[:, cs] * e_pair + dsk[:, cs] * xp
            xw_parts.append(xp * jnp.where(lane_lo, bcast_col(wg, r0), bcast_col(wg, r1)))
        xw = jnp.concatenate(xw_parts, axis=1).astype(bf16)
        upd = _dot(xw, bb, _TN)
        for r in range(HEADS_PER_GROUP):
            rs = slice(r * HEADDIM, (r + 1) * HEADDIM)
            cdr = jnp.broadcast_to(cdg[:, r:r + 1], (HEADDIM, D_STATE))
            state_ref[0, h0 + r] = s_old[rs] * cdr + upd[rs]
        return carry

    lax.fori_loop(0, N_GROUPS, group_body, 0)

    ss = jnp.zeros((q, 1), f32)
    for g in range(N_GROUPS):
        z = proj_ref[:, OFF_Z + g * GROUP_W:OFF_Z + (g + 1) * GROUP_W]
        v = y_s[g] * _silu(z)
        y_s[g] = v
        ss = ss + jnp.sum(v * v, axis=-1, keepdims=True)
    rn = lax.rsqrt(ss / D_SSD + EPS)
    for g in range(N_GROUPS):
        cs = slice(g * GROUP_W, (g + 1) * GROUP_W)
        mix_ref[:, D_SC + g * GROUP_W:D_SC + (g + 1) * GROUP_W] = ((y_s[g] * rn) * gssd_ref[:, cs]).astype(bf16)


def _mixer_prompt(proj, bp, seq, w_sc_conv, w_ssd_conv, b_ssd_conv, dt_bias, a_log, dexp, g_ssd_norm):
    nc = seq // CHUNK
    rows = bp * seq
    full = lambda shape: pl.BlockSpec(shape, lambda b, c: (0,) * len(shape))
    hp = HEADS_PER_GROUP
    return pl.pallas_call(
        _mixer_prompt_kernel,
        grid=(bp, nc),
        in_specs=[
            pl.BlockSpec((CHUNK, D_IN), lambda b, c: (b * nc + c, 0)),
            full((3, D_SC)), full((4, D_XBC)), full((1, D_XBC)), full((1, N_HEADS)), full((1, N_HEADS)),
            full((N_GROUPS, 1, GROUP_W)), full((1, D_SSD)),
        ],
        out_specs=[
            pl.BlockSpec((CHUNK, D_MIX), lambda b, c: (b * nc + c, 0)),
            pl.BlockSpec((1, 2, D_SC), lambda b, c: (b, 0, 0)),
            pl.BlockSpec((1, 3, D_XBC), lambda b, c: (b, 0, 0)),
            pl.BlockSpec((1, N_HEADS, HEADDIM, D_STATE), lambda b, c: (b, 0, 0, 0)),
        ],
        out_shape=[
            jax.ShapeDtypeStruct((rows, D_MIX), bf16),
            jax.ShapeDtypeStruct((bp, 2, D_SC), f32),
            jax.ShapeDtypeStruct((bp, 3, D_XBC), f32),
            jax.ShapeDtypeStruct((bp, N_HEADS, HEADDIM, D_STATE), f32),
        ],
        scratch_shapes=[
            pltpu.VMEM((8 + CHUNK, D_SC), f32),
            pltpu.VMEM((8 + CHUNK, D_XBC), f32),
            pltpu.VMEM((N_GROUPS, CHUNK, GROUP_W), f32),
            pltpu.VMEM((N_GROUPS, CHUNK, D_STATE), f32),
            pltpu.VMEM((N_GROUPS, CHUNK, D_STATE), f32),
            pltpu.VMEM((N_GROUPS, CHUNK, GROUP_W), f32),
            pltpu.VMEM((N_GROUPS, CHUNK, hp), f32),
            pltpu.VMEM((N_GROUPS, CHUNK, hp), f32),
            pltpu.VMEM((N_GROUPS, CHUNK, hp), f32),
            pltpu.VMEM((N_GROUPS, CHUNK, hp), f32),
            pltpu.VMEM((N_GROUPS, hp, CHUNK), f32),
            pltpu.VMEM((N_GROUPS, 1, hp), f32),
        ],
        compiler_params=_params("arbitrary", "arbitrary"),
        name="mixer_prompt",
    )(proj, w_sc_conv, w_ssd_conv, b_ssd_conv, dt_bias, a_log, dexp, g_ssd_norm)


S1_ROWS = 128


def _mixer_s1_kernel(proj_ref, scp_ref, ssdp_ref, wsc_ref, wssd_ref, bssd_ref, dtb_ref, alog_ref, dexp_ref,
                     gmat_ref, emat_ref,
                     ysc_ref, u_ref, ypart_ref, xw_ref, eacx_ref, b_ref, c_ref, cd_ref):
    n = S1_ROWS
    t = jnp.bitwise_and(lax.broadcasted_iota(jnp.int32, (n, 1), 0), DEC_SEQ - 1)

    def shifted(x, tail, s):
        return jnp.where(t >= s, pltpu.roll(x, s, 0), pltpu.roll(tail, n - DEC_SEQ + s, 0))

    def shifted0(x, s):
        return jnp.where(t >= s, pltpu.roll(x, s, 0), 0.0)

    u = proj_ref[:, OFF_C:OFF_C + D_SC] * proj_ref[:, OFF_X:OFF_X + D_SC]
    scp = scp_ref[...]
    uc = wsc_ref[2:3, :] * u + wsc_ref[1:2, :] * shifted(u, scp, 1) + wsc_ref[0:1, :] * shifted(u, scp, 2)
    ysc_ref[...] = (proj_ref[:, OFF_B:OFF_B + D_SC] * uc).astype(bf16)
    u_ref[...] = u

    xbc = proj_ref[:, OFF_XBC:OFF_XBC + D_XBC]
    tail = ssdp_ref[...]
    acc = (wssd_ref[3:4, :] * xbc + wssd_ref[2:3, :] * shifted(xbc, tail, 1)
           + wssd_ref[1:2, :] * shifted(xbc, tail, 2) + wssd_ref[0:1, :] * shifted(xbc, tail, 3) + bssd_ref[...])
    xc = _silu(acc)
    xs = xc[:, 0:D_SSD]
    bm = xc[:, D_SSD:D_SSD + D_BC]
    cm = xc[:, D_SSD + D_BC:D_XBC]
    b_ref[...] = bm
    c_ref[...] = cm

    dt = _softplus(proj_ref[:, OFF_DT:OFF_DT + N_HEADS] + dtb_ref[...])
    a = dt * (-jnp.exp(alog_ref[...]))
    ac = a + shifted0(a, 1) + shifted0(a, 2) + shifted0(a, 3)
    ac_last = jnp.where(t == DEC_SEQ - 1, ac, 0.0)
    for d in range(1, DEC_SEQ):
        ac_last = ac_last + jnp.where(t == DEC_SEQ - 1 - d, pltpu.roll(ac, n - d, 0), 0.0)
    wv = dt * jnp.exp(ac_last - ac)
    eac = jnp.exp(ac)
    cd_ref[...] = jnp.exp(ac_last)

    gmat = gmat_ref[...]
    emat = emat_ref[...]
    y = dexp_ref[...] * xs
    for s in range(DEC_SEQ):
        if s == 0:
            coef = _dot3(cm * bm, gmat) * dt
            x_s = xs
        else:
            live = t >= s
            cbh = _dot3(cm * pltpu.roll(bm, s, 0), gmat)
            dec = jnp.exp(jnp.where(live, ac - pltpu.roll(ac, s, 0), 0.0))
            coef = jnp.where(live, cbh * dec * pltpu.roll(dt, s, 0), 0.0)
            x_s = pltpu.roll(xs, s, 0)
        y = y + _dot3(coef, emat) * x_s
    ypart_ref[...] = y
    xw_ref[...] = xs * _dot3(wv, emat)
    eacx_ref[...] = _dot3(eac, emat)


def _mixer_s1(proj, scp, ssdp, w_sc_conv, w_ssd_conv, b_ssd_conv, dt_bias, a_log, dexp, gmat, emat):
    rows = proj.shape[0]
    n = S1_ROWS
    full = lambda shape: pl.BlockSpec(shape, lambda i: (0,) * len(shape))
    rowblk = lambda w: pl.BlockSpec((n, w), lambda i: (i, 0))
    return pl.pallas_call(
        _mixer_s1_kernel,
        grid=(rows // n,),
        in_specs=[
            rowblk(D_IN), rowblk(D_SC), rowblk(D_XBC),
            full((3, D_SC)), full((4, D_XBC)), full((1, D_XBC)), full((1, N_HEADS)), full((1, N_HEADS)),
            full((1, D_SSD)), full((D_BC, N_HEADS)), full((N_HEADS, D_SSD)),
        ],
        out_specs=[rowblk(D_SC), rowblk(D_SC), rowblk(D_SSD), rowblk(D_SSD), rowblk(D_SSD),
                   rowblk(D_BC), rowblk(D_BC), rowblk(N_HEADS)],
        out_shape=[
            jax.ShapeDtypeStruct((rows, D_SC), bf16),
            jax.ShapeDtypeStruct((rows, D_SC), f32),
            jax.ShapeDtypeStruct((rows, D_SSD), f32),
            jax.ShapeDtypeStruct((rows, D_SSD), f32),
            jax.ShapeDtypeStruct((rows, D_SSD), f32),
            jax.ShapeDtypeStruct((rows, D_BC), f32),
            jax.ShapeDtypeStruct((rows, D_BC), f32),
            jax.ShapeDtypeStruct((rows, N_HEADS), f32),
        ],
        compiler_params=_params("arbitrary"),
        name="mixer_s1",
    )(proj, scp, ssdp, w_sc_conv, w_ssd_conv, b_ssd_conv, dt_bias, a_log, dexp, gmat, emat)


def _mixer_s2_kernel(state_ref, proj_ref, ypart_ref, xw_ref, eacx_ref, b_ref, c_ref, cd_ref, ysc_ref, gssd_ref,
                     mix_ref, snew_ref):
    vs = []
    ss = jnp.zeros((DEC_SEQ, 1), f32)
    for g in range(N_GROUPS):
        cs = slice(g * GROUP_W, (g + 1) * GROUP_W)
        ns = slice(g * D_STATE, (g + 1) * D_STATE)
        h0 = g * HEADS_PER_GROUP
        s_old = state_ref[0, h0:h0 + HEADS_PER_GROUP].reshape(GROUP_W, D_STATE)
        cg = c_ref[0, :, ns].astype(bf16)
        bg = b_ref[0, :, ns].astype(bf16)
        yoff = _dot(cg, s_old.astype(bf16), _NT)
        upd = _dot(xw_ref[0, :, cs].astype(bf16), bg, _TN)
        for r in range(HEADS_PER_GROUP):
            rs = slice(r * HEADDIM, (r + 1) * HEADDIM)
            cdr = jnp.broadcast_to(cd_ref[0, 0:1, h0 + r:h0 + r + 1], (HEADDIM, D_STATE))
            snew_ref[0, h0 + r] = s_old[rs] * cdr + upd[rs]
        yg = ypart_ref[0, :, cs] + yoff * eacx_ref[0, :, cs]
        v = yg * _silu(proj_ref[0, :, OFF_Z + g * GROUP_W:OFF_Z + (g + 1) * GROUP_W])
        vs.append(v)
        ss = ss + jnp.sum(v * v, axis=-1, keepdims=True)
    rn = lax.rsqrt(ss / D_SSD + EPS)
    mix_ref[0, :, 0:D_SC] = ysc_ref[0]
    for g in range(N_GROUPS):
        cs = slice(g * GROUP_W, (g + 1) * GROUP_W)
        mix_ref[0, :, D_SC + g * GROUP_W:D_SC + (g + 1) * GROUP_W] = ((vs[g] * rn) * gssd_ref[:, cs]).astype(bf16)


def _mixer_s2(state, proj3, ypart, xw, eacx, bm, cm, cd, ysc, g_ssd_norm):
    bs = state.shape[0]
    seqblk = lambda w: pl.BlockSpec((1, DEC_SEQ, w), lambda i: (i, 0, 0))
    stblk = pl.BlockSpec((1, N_HEADS, HEADDIM, D_STATE), lambda i: (i, 0, 0, 0))
    return pl.pallas_call(
        _mixer_s2_kernel,
        grid=(bs,),
        in_specs=[stblk, seqblk(D_IN), seqblk(D_SSD), seqblk(D_SSD), seqblk(D_SSD), seqblk(D_BC), seqblk(D_BC),
                  seqblk(N_HEADS), seqblk(D_SC), pl.BlockSpec((1, D_SSD), lambda i: (0, 0))],
        out_specs=[seqblk(D_MIX), stblk],
        out_shape=[
            jax.ShapeDtypeStruct((bs, DEC_SEQ, D_MIX), bf16),
            jax.ShapeDtypeStruct((bs, N_HEADS, HEADDIM, D_STATE), f32),
        ],
        compiler_params=_params("arbitrary"),
        name="mixer_s2",
    )(state, proj3, ypart, xw, eacx, bm, cm, cd, ysc, g_ssd_norm)


FFN_W = D_FF // 2
FFN_ROWS = 128


def _ffn_prompt_kernel(ug_ref, uv_ref, wg_ref, wv_ref, bg_ref, bv_ref, act_ref, bufg_ref, bufv_ref, gwin, vwin):
    n = FFN_ROWS
    c = pl.program_id(2)

    @pl.when(c == 0)
    def _init():
        gwin[0:8, :] = jnp.zeros((8, FFN_W), f32)
        vwin[0:8, :] = jnp.zeros((8, FFN_W), f32)

    def conv(u_ref, win, w_ref, b_ref, buf_ref):
        u = u_ref[...]
        win[8:8 + n, :] = u
        out = w_ref[2:3, :] * u + w_ref[1:2, :] * win[7:7 + n, :] + w_ref[0:1, :] * win[6:6 + n, :] + b_ref[...]
        buf_ref[0] = win[n + 6:n + 8, :]
        win[0:8, :] = win[n:n + 8, :]
        return out

    gc = conv(ug_ref, gwin, wg_ref, bg_ref, bufg_ref)
    vc = conv(uv_ref, vwin, wv_ref, bv_ref, bufv_ref)
    act_ref[...] = (_silu(gc) * vc).astype(bf16)


def _ffn_prompt(up, bp, seq, w_ffn_conv, b_ffn_conv):
    n = FFN_ROWS
    nc = seq // n
    nj = D_FF // FFN_W
    rows = bp * seq
    return pl.pallas_call(
        _ffn_prompt_kernel,
        grid=(bp, nj, nc),
        in_specs=[
            pl.BlockSpec((n, FFN_W), lambda b, j, c: (b * nc + c, j)),
            pl.BlockSpec((n, FFN_W), lambda b, j, c: (b * nc + c, j + nj)),
            pl.BlockSpec((3, FFN_W), lambda b, j, c: (0, j)),
            pl.BlockSpec((3, FFN_W), lambda b, j, c: (0, j + nj)),
            pl.BlockSpec((1, FFN_W), lambda b, j, c: (0, j)),
            pl.BlockSpec((1, FFN_W), lambda b, j, c: (0, j + nj)),
        ],
        out_specs=[
            pl.BlockSpec((n, FFN_W), lambda b, j, c: (b * nc + c, j)),
            pl.BlockSpec((1, 2, FFN_W), lambda b, j, c: (b, 0, j)),
            pl.BlockSpec((1, 2, FFN_W), lambda b, j, c: (b, 0, j)),
        ],
        out_shape=[
            jax.ShapeDtypeStruct((rows, D_FF), bf16),
            jax.ShapeDtypeStruct((bp, 2, D_FF), f32),
            jax.ShapeDtypeStruct((bp, 2, D_FF), f32),
        ],
        scratch_shapes=[pltpu.VMEM((8 + n, FFN_W), f32), pltpu.VMEM((8 + n, FFN_W), f32)],
        compiler_params=_params("arbitrary", "arbitrary", "arbitrary"),
        name="ffn_prompt",
    )(up, up, w_ffn_conv, w_ffn_conv, b_ffn_conv, b_ffn_conv)


def _ffn_sample_kernel(ug_ref, uv_ref, pg_ref, pv_ref, wg_ref, wv_ref, bg_ref, bv_ref, act_ref):
    n = FFN_ROWS
    t = jnp.bitwise_and(lax.broadcasted_iota(jnp.int32, (n, 1), 0), DEC_SEQ - 1)

    def conv(u_ref, p_ref, w_ref, b_ref):
        u = u_ref[...]
        tail = p_ref[...]
        out = w_ref[2:3, :] * u + b_ref[...]
        for s in (1, 2):
            sh = jnp.where(t >= s, pltpu.roll(u, s, 0), pltpu.roll(tail, n - DEC_SEQ + s, 0))
            out = out + w_ref[2 - s:3 - s, :] * sh
        return out

    gc = conv(ug_ref, pg_ref, wg_ref, bg_ref)
    vc = conv(uv_ref, pv_ref, wv_ref, bv_ref)
    act_ref[...] = (_silu(gc) * vc).astype(bf16)


def _ffn_sample(up, tailp, w_ffn_conv, b_ffn_conv):
    n = FFN_ROWS
    nj = D_FF // FFN_W
    rows = up.shape[0]
    return pl.pallas_call(
        _ffn_sample_kernel,
        grid=(rows // n, nj),
        in_specs=[
            pl.BlockSpec((n, FFN_W), lambda i, j: (i, j)),
            pl.BlockSpec((n, FFN_W), lambda i, j: (i, j + nj)),
            pl.BlockSpec((n, FFN_W), lambda i, j: (i, j)),
            pl.BlockSpec((n, FFN_W), lambda i, j: (i, j + nj)),
            pl.BlockSpec((3, FFN_W), lambda i, j: (0, j)),
            pl.BlockSpec((3, FFN_W), lambda i, j: (0, j + nj)),
            pl.BlockSpec((1, FFN_W), lambda i, j: (0, j)),
            pl.BlockSpec((1, FFN_W), lambda i, j: (0, j + nj)),
        ],
        out_specs=pl.BlockSpec((n, FFN_W), lambda i, j: (i, j)),
        out_shape=jax.ShapeDtypeStruct((rows, D_FF), bf16),
        compiler_params=_params("arbitrary", "arbitrary"),
        name="ffn_sample",
    )(up, up, tailp, tailp, w_ffn_conv, w_ffn_conv, b_ffn_conv, b_ffn_conv)


def _tile(rows, pref):
    t = min(rows, pref)
    assert rows % t == 0, (rows, pref)
    return t


def _pad_tail(buf):
    b, k1, c = buf.shape
    return jnp.pad(buf, ((0, 0), (DEC_SEQ - k1, 0), (0, 0))).reshape(b * DEC_SEQ, c)


def kernel(x_prompt, x_sample, c_prompt, c_sample, state_sc_conv, state_ssd_conv, state_ssm, state_ffn_conv, w_ada, b_ada, g_norm1, w_in, w_sc_conv, w_ssd_conv, b_ssd_conv, dt_bias, a_log, d_skip, g_ssd_norm, w_out, g_norm2, w_up, w_ffn_conv, b_ffn_conv, w_down, g_final):
    bp, seq, d = x_prompt.shape
    bs, ls, _ = x_sample.shape
    assert d == D_MODEL and ls == DEC_SEQ and bp <= 8 and seq % CHUNK == 0 and (bs * ls) % S1_ROWS == 0
    assert w_ada.shape[0] == 1, "single-layer trunk"
    rp, rs = bp * seq, bs * ls

    xp = x_prompt.reshape(rp, d)
    xs = x_sample.reshape(rs, d)

    c_all = jnp.concatenate([c_prompt, jnp.zeros((8 - bp, d), f32), c_sample], axis=0)
    mod = _ada(c_all, w_ada[0], b_ada[0].reshape(1, -1))
    mod_p = [mod[k, :bp].reshape(bp, 1, d) for k in range(N_MOD)]
    mod_s = [jnp.repeat(mod[k, 8:], ls, axis=0) for k in range(N_MOD)]

    row2 = lambda v: v.reshape(1, -1)
    g1, g2, gf, gssd = row2(g_norm1[0]), row2(g_norm2[0]), row2(g_final), row2(g_ssd_norm[0])
    dtb, alog = row2(dt_bias[0]), row2(a_log[0])
    dexp_flat = jnp.repeat(d_skip[0], HEADDIM).reshape(1, D_SSD)
    dexp_grp = dexp_flat.reshape(N_GROUPS, 1, GROUP_W)
    bssd = row2(b_ssd_conv[0])
    bffn = row2(b_ffn_conv[0])
    gmat = (jnp.arange(D_BC)[:, None] // D_STATE == jnp.arange(N_HEADS)[None, :] // HEADS_PER_GROUP).astype(bf16)
    emat = (jnp.arange(N_HEADS)[:, None] == jnp.arange(D_SSD)[None, :] // HEADDIM).astype(bf16)

    tm_p = _tile(seq, 2048)
    tm_p2 = _tile(seq, 1024)
    tn_norm = _tile(seq, 512)

    h = _normmod(xp, g1, mod_p[1], mod_p[0], tn_norm, seq)
    proj = _matmul(h, w_in[0], tm_p, 512, "in_proj_p")
    mix, p_sc, p_ssd, p_ssm = _mixer_prompt(proj, bp, seq, w_sc_conv[0], w_ssd_conv[0], bssd, dtb, alog,
                                            dexp_grp, gssd)
    x1 = _matmul_res(mix, w_out[0], xp, mod_p[2], tm_p2, 512, seq, "out_proj_p")
    h2 = _normmod(x1, g2, mod_p[4], mod_p[3], tn_norm, seq)
    up = _matmul(h2, w_up[0], tm_p, 512, "up_proj_p")
    act, p_fg, p_fv = _ffn_prompt(up, bp, seq, w_ffn_conv[0], bffn)
    x2 = _matmul_res(act, w_down[0], x1, mod_p[5], tm_p2, 256, seq, "down_proj_p")
    y_prompt = _final_norm(x2, gf, tn_norm).reshape(bp, seq, d)
    p_ffn = jnp.concatenate([p_fg, p_fv], axis=-1)

    hs = _normmod(xs, g1, mod_s[1], mod_s[0], rs, rs)
    proj_s = _matmul(hs, w_in[0], rs, 512, "in_proj_s")
    ysc, u_s, ypart, xw, eacx, bm, cm, cd = _mixer_s1(
        proj_s, _pad_tail(state_sc_conv[0]), _pad_tail(state_ssd_conv[0]), w_sc_conv[0], w_ssd_conv[0], bssd,
        dtb, alog, dexp_flat, gmat, emat)
    r3 = lambda v: v.reshape(bs, ls, v.shape[-1])
    mix_s, s_ssm = _mixer_s2(state_ssm[0], r3(proj_s), r3(ypart), r3(xw), r3(eacx), r3(bm), r3(cm), r3(cd),
                             r3(ysc), gssd)
    x1s = _matmul_res(mix_s.reshape(rs, D_MIX), w_out[0], xs, mod_s[2], rs, 512, rs, "out_proj_s")
    h2s = _normmod(x1s, g2, mod_s[4], mod_s[3], rs, rs)
    up_s = _matmul(h2s, w_up[0], rs, 512, "up_proj_s")
    act_s = _ffn_sample(up_s, _pad_tail(state_ffn_conv[0]), w_ffn_conv[0], bffn)
    x2s = _matmul_res(act_s, w_down[0], x1s, mod_s[5], rs, 256, rs, "down_proj_s")
    y_sample = _final_norm(x2s, gf, rs).reshape(bs, ls, d)

    s_sc = r3(u_s)[:, ls - 2:]
    s_ssd = r3(proj_s)[:, ls - 3:, OFF_XBC:OFF_XBC + D_XBC]
    s_ffn = r3(up_s)[:, ls - 2:]

    lead = lambda v: v[None]
    return (y_prompt, y_sample, lead(p_sc), lead(p_ssd), lead(p_ssm), lead(p_ffn),
            lead(s_sc), lead(s_ssd), lead(s_ssm), lead(s_ffn))
```

```python
import functools

import jax
import jax.numpy as jnp
from jax import lax
from jax.experimental import pallas as pl
from jax.experimental.pallas import tpu as pltpu

EPS = 1e-6
D_MODEL = 4096
D_SC = 2048
D_SSD = 4096
N_HEADS = 64
HEADDIM = 64
N_GROUPS = 8
HEADS_PER_GROUP = N_HEADS // N_GROUPS
D_STATE = 128
GROUP_W = HEADS_PER_GROUP * HEADDIM
D_BC = N_GROUPS * D_STATE
D_XBC = D_SSD + 2 * D_BC
D_MIX = D_SC + D_SSD
D_IN = 3 * D_SC + D_SSD + D_XBC + N_HEADS
D_FF = 11008
N_MOD = 6
CHUNK = 128
DEC_SEQ = 4

OFF_B, OFF_C, OFF_X, OFF_Z, OFF_XBC, OFF_DT = 0, 2048, 4096, 6144, 10240, 16384

VMEM_LIMIT_BYTES = 56 * 1024 * 1024

_NN = (((1,), (0,)), ((), ()))
_NT = (((1,), (1,)), ((), ()))
_TN = (((0,), (0,)), ((), ()))

f32 = jnp.float32
bf16 = jnp.bfloat16


def _dot(a, b, dims=_NN):
    return lax.dot_general(a, b, dims, preferred_element_type=f32)


def _split3(x):
    hi = x.astype(bf16)
    r1 = x - hi.astype(f32)
    mid = r1.astype(bf16)
    lo = (r1 - mid.astype(f32)).astype(bf16)
    return hi, mid, lo


def _dot3(x, m):
    hi, mid, lo = _split3(x)
    return _dot(hi, m) + _dot(mid, m) + _dot(lo, m)


def _dot3r(m, x):
    hi, mid, lo = _split3(x)
    return _dot(m, hi) + _dot(m, mid) + _dot(m, lo)


def _silu(x):
    return x * jax.nn.sigmoid(x)


def _softplus(x):
    return jnp.maximum(x, 0.0) + jnp.log1p(jnp.exp(-jnp.abs(x)))


def _params(*sem):
    return pltpu.CompilerParams(dimension_semantics=sem, vmem_limit_bytes=VMEM_LIMIT_BYTES)


def _ada_kernel(c_ref, w_ref, b_ref, o_ref):
    a = _silu(c_ref[...]).astype(bf16)
    o_ref[0] = _dot(a, w_ref[...]) + b_ref[...]


def _ada(c_all, w_ada, b_ada):
    rc = c_all.shape[0]
    tn = 512
    nb = D_MODEL // tn
    return pl.pallas_call(
        _ada_kernel,
        grid=(N_MOD * nb,),
        in_specs=[
            pl.BlockSpec((rc, D_MODEL), lambda j: (0, 0)),
            pl.BlockSpec((D_MODEL, tn), lambda j: (0, j)),
            pl.BlockSpec((1, tn), lambda j: (0, j)),
        ],
        out_specs=pl.BlockSpec((1, rc, tn), lambda j: (j // nb, 0, j % nb)),
        out_shape=jax.ShapeDtypeStruct((N_MOD, rc, D_MODEL), f32),
        compiler_params=_params("arbitrary"),
        name="ada",
    )(c_all, w_ada, b_ada)


def _mod_spec(mod, tm, tn, rows_per_seq, ndim_grid):
    if mod.ndim == 3:
        tiles_per_seq = rows_per_seq // tm
        if ndim_grid == 1:
            return pl.BlockSpec((None, 1, tn), lambda i: (i // tiles_per_seq, 0, 0))
        return pl.BlockSpec((None, 1, tn), lambda i, j: (i // tiles_per_seq, 0, j))
    if ndim_grid == 1:
        return pl.BlockSpec((tm, tn), lambda i: (i, 0))
    return pl.BlockSpec((tm, tn), lambda i, j: (i, j))


def _normmod_kernel(x_ref, g_ref, sc_ref, sh_ref, o_ref):
    x = x_ref[...]
    r = lax.rsqrt(jnp.mean(x * x, axis=-1, keepdims=True) + EPS)
    h = (x * r) * g_ref[...]
    o_ref[...] = (h * (1.0 + sc_ref[...]) + sh_ref[...]).astype(o_ref.dtype)


def _normmod(x, g, scale, shift, tm, rows_per_seq):
    rows = x.shape[0]
    return pl.pallas_call(
        _normmod_kernel,
        grid=(rows // tm,),
        in_specs=[
            pl.BlockSpec((tm, D_MODEL), lambda i: (i, 0)),
            pl.BlockSpec((1, D_MODEL), lambda i: (0, 0)),
            _mod_spec(scale, tm, D_MODEL, rows_per_seq, 1),
            _mod_spec(shift, tm, D_MODEL, rows_per_seq, 1),
        ],
        out_specs=pl.BlockSpec((tm, D_MODEL), lambda i: (i, 0)),
        out_shape=jax.ShapeDtypeStruct((rows, D_MODEL), bf16),
        compiler_params=_params("arbitrary"),
        name="normmod",
    )(x, g, scale, shift)


def _norm_kernel(x_ref, g_ref, o_ref):
    x = x_ref[...]
    r = lax.rsqrt(jnp.mean(x * x, axis=-1, keepdims=True) + EPS)
    o_ref[...] = (x * r) * g_ref[...]


def _final_norm(x, g, tm):
    rows = x.shape[0]
    return pl.pallas_call(
        _norm_kernel,
        grid=(rows // tm,),
        in_specs=[
            pl.BlockSpec((tm, D_MODEL), lambda i: (i, 0)),
            pl.BlockSpec((1, D_MODEL), lambda i: (0, 0)),
        ],
        out_specs=pl.BlockSpec((tm, D_MODEL), lambda i: (i, 0)),
        out_shape=jax.ShapeDtypeStruct((rows, D_MODEL), f32),
        compiler_params=_params("arbitrary"),
        name="final_norm",
    )(x, g)


def _mm_kernel(a_ref, w_ref, o_ref):
    o_ref[...] = _dot(a_ref[...], w_ref[...])


def _mm_nt_kernel(a_ref, wt_ref, o_ref):
    o_ref[...] = _dot(a_ref[...], wt_ref[...], _NT)


def _mm_res_kernel(a_ref, w_ref, x_ref, g_ref, o_ref):
    o_ref[...] = x_ref[...] + g_ref[...] * _dot(a_ref[...], w_ref[...])


def _matmul(a, w, tm, tn, name):
    m, k = a.shape
    n = w.shape[1]
    return pl.pallas_call(
        _mm_kernel,
        grid=(m // tm, pl.cdiv(n, tn)),
        in_specs=[
            pl.BlockSpec((tm, k), lambda i, j: (i, 0), pipeline_mode=pl.Buffered(1)),
            pl.BlockSpec((k, tn), lambda i, j: (0, j)),
        ],
        out_specs=pl.BlockSpec((tm, tn), lambda i, j: (i, j)),
        out_shape=jax.ShapeDtypeStruct((m, n), f32),
        compiler_params=_params("arbitrary", "arbitrary"),
        name=name,
    )(a, w)


def _matmul_nt(a, wt, tm, tn, name):
    m, k = a.shape
    n = wt.shape[0]
    return pl.pallas_call(
        _mm_nt_kernel,
        grid=(m // tm, pl.cdiv(n, tn)),
        in_specs=[
            pl.BlockSpec((tm, k), lambda i, j: (i, 0), pipeline_mode=pl.Buffered(1)),
            pl.BlockSpec((tn, k), lambda i, j: (j, 0)),
        ],
        out_specs=pl.BlockSpec((tm, tn), lambda i, j: (i, j)),
        out_shape=jax.ShapeDtypeStruct((m, n), f32),
        compiler_params=_params("arbitrary", "arbitrary"),
        name=name,
    )(a, wt)


def _matmul_res(a, w, x, gate, tm, tn, rows_per_seq, name):
    m, k = a.shape
    n = w.shape[1]
    return pl.pallas_call(
        _mm_res_kernel,
        grid=(m // tm, n // tn),
        in_specs=[
            pl.BlockSpec((tm, k), lambda i, j: (i, 0), pipeline_mode=pl.Buffered(1)),
            pl.BlockSpec((k, tn), lambda i, j: (0, j)),
            pl.BlockSpec((tm, tn), lambda i, j: (i, j)),
            _mod_spec(gate, tm, tn, rows_per_seq, 2),
        ],
        out_specs=pl.BlockSpec((tm, tn), lambda i, j: (i, j)),
        out_shape=jax.ShapeDtypeStruct((m, n), f32),
        compiler_params=_params("arbitrary", "arbitrary"),
        name=name,
    )(a, w, x, gate)


def _mixer_prompt_kernel(proj_ref, wsc_ref, wssd_ref, bssd_ref, dtb_ref, alog_ref, dexp_ref, gssd_ref,
                         mix_ref, scbuf_ref, ssdbuf_ref, state_ref,
                         uwin, xwin, xs_s, b_s, c_s, y_s, ac_s, dt_s, w_s, eac_s, act_s, cd_s):
    q = CHUNK
    c = pl.program_id(1)

    @pl.when(c == 0)
    def _init():
        uwin[0:8, :] = jnp.zeros((8, D_SC), f32)
        xwin[0:8, :] = jnp.zeros((8, D_XBC), f32)
        state_ref[...] = jnp.zeros(state_ref.shape, f32)

    u = proj_ref[:, OFF_C:OFF_C + D_SC] * proj_ref[:, OFF_X:OFF_X + D_SC]
    uwin[8:8 + q, :] = u
    uc = (wsc_ref[2:3, :] * u + wsc_ref[1:2, :] * uwin[7:7 + q, :] + wsc_ref[0:1, :] * uwin[6:6 + q, :])
    mix_ref[:, 0:D_SC] = (proj_ref[:, OFF_B:OFF_B + D_SC] * uc).astype(bf16)
    scbuf_ref[0] = uwin[q + 6:q + 8, :]
    uwin[0:8, :] = uwin[q:q + 8, :]

    xwin[8:8 + q, :] = proj_ref[:, OFF_XBC:OFF_XBC + D_XBC]

    def conv_slab(lo, width):
        acc = (wssd_ref[3:4, lo:lo + width] * xwin[8:8 + q, lo:lo + width]
               + wssd_ref[2:3, lo:lo + width] * xwin[7:7 + q, lo:lo + width]
               + wssd_ref[1:2, lo:lo + width] * xwin[6:6 + q, lo:lo + width]
               + wssd_ref[0:1, lo:lo + width] * xwin[5:5 + q, lo:lo + width]
               + bssd_ref[:, lo:lo + width])
        return _silu(acc)

    for g in range(N_GROUPS):
        xs_s[g] = conv_slab(g * GROUP_W, GROUP_W)
        b_s[g] = conv_slab(D_SSD + g * D_STATE, D_STATE)
        c_s[g] = conv_slab(D_SSD + D_BC + g * D_STATE, D_STATE)
    ssdbuf_ref[0] = xwin[q + 5:q + 8, :]
    xwin[0:8, :] = xwin[q:q + 8, :]

    dt = _softplus(proj_ref[:, OFF_DT:OFF_DT + N_HEADS] + dtb_ref[...])
    a = dt * (-jnp.exp(alog_ref[...]))
    row = lax.broadcasted_iota(jnp.int32, (q, q), 0)
    col = lax.broadcasted_iota(jnp.int32, (q, q), 1)
    tri = col <= row
    ac = _dot3r(jnp.where(tri, 1.0, 0.0).astype(bf16), a)
    ac_last = ac[q - 1:q, :]
    wv = dt * jnp.exp(ac_last - ac)
    eac = jnp.exp(ac)
    cd = jnp.exp(ac_last)
    ac_t = ac.T
    for g in range(N_GROUPS):
        hs = slice(g * HEADS_PER_GROUP, (g + 1) * HEADS_PER_GROUP)
        ac_s[g] = ac[:, hs]
        dt_s[g] = dt[:, hs]
        w_s[g] = wv[:, hs]
        eac_s[g] = eac[:, hs]
        cd_s[g] = cd[:, hs]
        act_s[g] = ac_t[hs, :]

    lane_lo = lax.broadcasted_iota(jnp.int32, (q, 2 * HEADDIM), 1) < HEADDIM

    def bcast_col(arr, r):
        return jnp.broadcast_to(arr[:, r:r + 1], (q, 2 * HEADDIM))

    def group_body(g, carry):
        bb = b_s[g].astype(bf16)
        cb_ = c_s[g].astype(bf16)
        cb = _dot(cb_, bb, _NT)
        h0 = pl.multiple_of(g * HEADS_PER_GROUP, HEADS_PER_GROUP)
        s_old = state_ref[0, pl.ds(h0, HEADS_PER_GROUP)].reshape(GROUP_W, D_STATE)
        yoff = _dot(cb_, s_old.astype(bf16), _NT)
        xg = xs_s[g]
        acg, dtg, wg, eg, atg, cdg = ac_s[g], dt_s[g], w_s[g], eac_s[g], act_s[g], cd_s[g]
        dsk = dexp_ref[g]
        xw_parts = []
        for pair in range(HEADS_PER_GROUP // 2):
            cs = slice(pair * 2 * HEADDIM, (pair + 1) * 2 * HEADDIM)
            r0, r1 = 2 * pair, 2 * pair + 1
            xp = xg[:, cs]
            xdt = xp * jnp.where(lane_lo, bcast_col(dtg, r0), bcast_col(dtg, r1))
            ydiag = jnp.zeros((q, 2 * HEADDIM), f32)
            for r, keep in ((r0, lane_lo), (r1, jnp.logical_not(lane_lo))):
                seg = bcast_col(acg, r) - jnp.broadcast_to(atg[r:r + 1, :], (q, q))
                decay = jnp.where(tri, jnp.exp(jnp.where(tri, seg, 0.0)), 0.0)
                m = (cb * decay).astype(bf16)
                ydiag = ydiag + _dot(m, jnp.where(keep, xdt, 0.0).astype(bf16))
            e_pair = jnp.where(lane_lo, bcast_col(eg, r0), bcast_col(eg, r1))
            y_s[g, :, cs] = ydiag + yoff[:, cs] * e_pair + dsk[:, cs] * xp
            xw_parts.append(xp * jnp.where(lane_lo, bcast_col(wg, r0), bcast_col(wg, r1)))
        xw = jnp.concatenate(xw_parts, axis=1).astype(bf16)
        upd = _dot(xw, bb, _TN)
        for r in range(HEADS_PER_GROUP):
            rs = slice(r * HEADDIM, (r + 1) * HEADDIM)
            cdr = jnp.broadcast_to(cdg[:, r:r + 1], (HEADDIM, D_STATE))
            state_ref[0, h0 + r] = s_old[rs] * cdr + upd[rs]
        return carry

    lax.fori_loop(0, N_GROUPS, group_body, 0)

    ss = jnp.zeros((q, 1), f32)
    for g in range(N_GROUPS):
        z = proj_ref[:, OFF_Z + g * GROUP_W:OFF_Z + (g + 1) * GROUP_W]
        v = y_s[g] * _silu(z)
        y_s[g] = v
        ss = ss + jnp.sum(v * v, axis=-1, keepdims=True)
    rn = lax.rsqrt(ss / D_SSD + EPS)
    for g in range(N_GROUPS):
        cs = slice(g * GROUP_W, (g + 1) * GROUP_W)
        mix_ref[:, D_SC + g * GROUP_W:D_SC + (g + 1) * GROUP_W] = ((y_s[g] * rn) * gssd_ref[:, cs]).astype(bf16)


def _mixer_prompt(proj, bp, seq, w_sc_conv, w_ssd_conv, b_ssd_conv, dt_bias, a_log, dexp, g_ssd_norm):
    nc = seq // CHUNK
    rows = bp * seq
    full = lambda shape: pl.BlockSpec(shape, lambda b, c: (0,) * len(shape))
    hp = HEADS_PER_GROUP
    return pl.pallas_call(
        _mixer_prompt_kernel,
        grid=(bp, nc),
        in_specs=[
            pl.BlockSpec((CHUNK, D_IN), lambda b, c: (b * nc + c, 0)),
            full((3, D_SC)), full((4, D_XBC)), full((1, D_XBC)), full((1, N_HEADS)), full((1, N_HEADS)),
            full((N_GROUPS, 1, GROUP_W)), full((1, D_SSD)),
        ],
        out_specs=[
            pl.BlockSpec((CHUNK, D_MIX), lambda b, c: (b * nc + c, 0)),
            pl.BlockSpec((1, 2, D_SC), lambda b, c: (b, 0, 0)),
            pl.BlockSpec((1, 3, D_XBC), lambda b, c: (b, 0, 0)),
            pl.BlockSpec((1, N_HEADS, HEADDIM, D_STATE), lambda b, c: (b, 0, 0, 0)),
        ],
        out_shape=[
            jax.ShapeDtypeStruct((rows, D_MIX), bf16),
            jax.ShapeDtypeStruct((bp, 2, D_SC), f32),
            jax.ShapeDtypeStruct((bp, 3, D_XBC), f32),
            jax.ShapeDtypeStruct((bp, N_HEADS, HEADDIM, D_STATE), f32),
        ],
        scratch_shapes=[
            pltpu.VMEM((8 + CHUNK, D_SC), f32),
            pltpu.VMEM((8 + CHUNK, D_XBC), f32),
            pltpu.VMEM((N_GROUPS, CHUNK, GROUP_W), f32),
            pltpu.VMEM((N_GROUPS, CHUNK, D_STATE), f32),
            pltpu.VMEM((N_GROUPS, CHUNK, D_STATE), f32),
            pltpu.VMEM((N_GROUPS, CHUNK, GROUP_W), f32),
            pltpu.VMEM((N_GROUPS, CHUNK, hp), f32),
            pltpu.VMEM((N_GROUPS, CHUNK, hp), f32),
            pltpu.VMEM((N_GROUPS, CHUNK, hp), f32),
            pltpu.VMEM((N_GROUPS, CHUNK, hp), f32),
            pltpu.VMEM((N_GROUPS, hp, CHUNK), f32),
            pltpu.VMEM((N_GROUPS, 1, hp), f32),
        ],
        compiler_params=_params("arbitrary", "arbitrary"),
        name="mixer_prompt",
    )(proj, w_sc_conv, w_ssd_conv, b_ssd_conv, dt_bias, a_log, dexp, g_ssd_norm)


S1_ROWS = 128


def _mixer_s1_kernel(proj_ref, scp_ref, ssdp_ref, wsc_ref, wssd_ref, bssd_ref, dtb_ref, alog_ref, dexp_ref,
                     gmat_ref, emat_ref,
                     ysc_ref, u_ref, ypart_ref, xw_ref, eacx_ref, b_ref, c_ref, cd_ref):
    n = S1_ROWS
    t = jnp.bitwise_and(lax.broadcasted_iota(jnp.int32, (n, 1), 0), DEC_SEQ - 1)

    def shifted(x, tail, s):
        return jnp.where(t >= s, pltpu.roll(x, s, 0), pltpu.roll(tail, n - DEC_SEQ + s, 0))

    def shifted0(x, s):
        return jnp.where(t >= s, pltpu.roll(x, s, 0), 0.0)

    u = proj_ref[:, OFF_C:OFF_C + D_SC] * proj_ref[:, OFF_X:OFF_X + D_SC]
    scp = scp_ref[...]
    uc = wsc_ref[2:3, :] * u + wsc_ref[1:2, :] * shifted(u, scp, 1) + wsc_ref[0:1, :] * shifted(u, scp, 2)
    ysc_ref[...] = (proj_ref[:, OFF_B:OFF_B + D_SC] * uc).astype(bf16)
    u_ref[...] = u

    xbc = proj_ref[:, OFF_XBC:OFF_XBC + D_XBC]
    tail = ssdp_ref[...]
    acc = (wssd_ref[3:4, :] * xbc + wssd_ref[2:3, :] * shifted(xbc, tail, 1)
           + wssd_ref[1:2, :] * shifted(xbc, tail, 2) + wssd_ref[0:1, :] * shifted(xbc, tail, 3) + bssd_ref[...])
    xc = _silu(acc)
    xs = xc[:, 0:D_SSD]
    bm = xc[:, D_SSD:D_SSD + D_BC]
    cm = xc[:, D_SSD + D_BC:D_XBC]
    b_ref[...] = bm
    c_ref[...] = cm

    dt = _softplus(proj_ref[:, OFF_DT:OFF_DT + N_HEADS] + dtb_ref[...])
    a = dt * (-jnp.exp(alog_ref[...]))
    ac = a + shifted0(a, 1) + shifted0(a, 2) + shifted0(a, 3)
    ac_last = jnp.where(t == DEC_SEQ - 1, ac, 0.0)
    for d in range(1, DEC_SEQ):
        ac_last = ac_last + jnp.where(t == DEC_SEQ - 1 - d, pltpu.roll(ac, n - d, 0), 0.0)
    wv = dt * jnp.exp(ac_last - ac)
    eac = jnp.exp(ac)
    cd_ref[...] = jnp.exp(ac_last)

    gmat = gmat_ref[...]
    emat = emat_ref[...]
    y = dexp_ref[...] * xs
    for s in range(DEC_SEQ):
        if s == 0:
            coef = _dot3(cm * bm, gmat) * dt
            x_s = xs
        else:
            live = t >= s
            cbh = _dot3(cm * pltpu.roll(bm, s, 0), gmat)
            dec = jnp.exp(jnp.where(live, ac - pltpu.roll(ac, s, 0), 0.0))
            coef = jnp.where(live, cbh * dec * pltpu.roll(dt, s, 0), 0.0)
            x_s = pltpu.roll(xs, s, 0)
        y = y + _dot3(coef, emat) * x_s
    ypart_ref[...] = y
    xw_ref[...] = xs * _dot3(wv, emat)
    eacx_ref[...] = _dot3(eac, emat)


def _mixer_s1(proj, scp, ssdp, w_sc_conv, w_ssd_conv, b_ssd_conv, dt_bias, a_log, dexp, gmat, emat):
    rows = proj.shape[0]
    n = S1_ROWS
    full = lambda shape: pl.BlockSpec(shape, lambda i: (0,) * len(shape))
    rowblk = lambda w: pl.BlockSpec((n, w), lambda i: (i, 0))
    return pl.pallas_call(
        _mixer_s1_kernel,
        grid=(rows // n,),
        in_specs=[
            rowblk(D_IN), rowblk(D_SC), rowblk(D_XBC),
            full((3, D_SC)), full((4, D_XBC)), full((1, D_XBC)), full((1, N_HEADS)), full((1, N_HEADS)),
            full((1, D_SSD)), full((D_BC, N_HEADS)), full((N_HEADS, D_SSD)),
        ],
        out_specs=[rowblk(D_SC), rowblk(D_SC), rowblk(D_SSD), rowblk(D_SSD), rowblk(D_SSD),
                   rowblk(D_BC), rowblk(D_BC), rowblk(N_HEADS)],
        out_shape=[
            jax.ShapeDtypeStruct((rows, D_SC), bf16),
            jax.ShapeDtypeStruct((rows, D_SC), f32),
            jax.ShapeDtypeStruct((rows, D_SSD), f32),
            jax.ShapeDtypeStruct((rows, D_SSD), f32),
            jax.ShapeDtypeStruct((rows, D_SSD), f32),
            jax.ShapeDtypeStruct((rows, D_BC), f32),
            jax.ShapeDtypeStruct((rows, D_BC), f32),
            jax.ShapeDtypeStruct((rows, N_HEADS), f32),
        ],
        compiler_params=_params("arbitrary"),
        name="mixer_s1",
    )(proj, scp, ssdp, w_sc_conv, w_ssd_conv, b_ssd_conv, dt_bias, a_log, dexp, gmat, emat)


S2_SEQS = 4
S2_ROWS = S2_SEQS * DEC_SEQ


def _mixer_s2_kernel(state_ref, proj_ref, ypart_ref, xw_ref, eacx_ref, b_ref, c_ref, cd_ref, ysc_ref, gssd_ref,
                     mix_ref, snew_ref):
    mix_ref[:, 0:D_SC] = ysc_ref[...]
    for e in range(S2_SEQS):
        tr = slice(e * DEC_SEQ, (e + 1) * DEC_SEQ)
        vs = []
        ss = jnp.zeros((DEC_SEQ, 1), f32)
        for g in range(N_GROUPS):
            cs = slice(g * GROUP_W, (g + 1) * GROUP_W)
            ns = slice(g * D_STATE, (g + 1) * D_STATE)
            h0 = g * HEADS_PER_GROUP
            s_old = state_ref[e, h0:h0 + HEADS_PER_GROUP].reshape(GROUP_W, D_STATE)
            cg = c_ref[tr, ns].astype(bf16)
            bg = b_ref[tr, ns].astype(bf16)
            yoff = _dot(cg, s_old.astype(bf16), _NT)
            upd = _dot(xw_ref[tr, cs].astype(bf16), bg, _TN)
            for r in range(HEADS_PER_GROUP):
                rs = slice(r * HEADDIM, (r + 1) * HEADDIM)
                row0 = e * DEC_SEQ
                cdr = jnp.broadcast_to(cd_ref[row0:row0 + 1, h0 + r:h0 + r + 1], (HEADDIM, D_STATE))
                snew_ref[e, h0 + r] = s_old[rs] * cdr + upd[rs]
            yg = ypart_ref[tr, cs] + yoff * eacx_ref[tr, cs]
            v = yg * _silu(proj_ref[tr, OFF_Z + g * GROUP_W:OFF_Z + (g + 1) * GROUP_W])
            vs.append(v)
            ss = ss + jnp.sum(v * v, axis=-1, keepdims=True)
        rn = lax.rsqrt(ss / D_SSD + EPS)
        for g in range(N_GROUPS):
            cs = slice(g * GROUP_W, (g + 1) * GROUP_W)
            mix_ref[tr, D_SC + g * GROUP_W:D_SC + (g + 1) * GROUP_W] = (
                (vs[g] * rn) * gssd_ref[:, cs]).astype(bf16)


def _mixer_s2(state, proj, ypart, xw, eacx, bm, cm, cd, ysc, g_ssd_norm):
    bs = state.shape[0]
    rowblk = lambda w: pl.BlockSpec((S2_ROWS, w), lambda i: (i, 0))
    stblk = pl.BlockSpec((S2_SEQS, N_HEADS, HEADDIM, D_STATE), lambda i: (i, 0, 0, 0))
    return pl.pallas_call(
        _mixer_s2_kernel,
        grid=(bs // S2_SEQS,),
        in_specs=[stblk, rowblk(D_IN), rowblk(D_SSD), rowblk(D_SSD), rowblk(D_SSD), rowblk(D_BC), rowblk(D_BC),
                  rowblk(N_HEADS), rowblk(D_SC), pl.BlockSpec((1, D_SSD), lambda i: (0, 0))],
        out_specs=[rowblk(D_MIX), stblk],
        out_shape=[
            jax.ShapeDtypeStruct((bs * DEC_SEQ, D_MIX), bf16),
            jax.ShapeDtypeStruct((bs, N_HEADS, HEADDIM, D_STATE), f32),
        ],
        compiler_params=_params("arbitrary"),
        name="mixer_s2",
    )(state, proj, ypart, xw, eacx, bm, cm, cd, ysc, g_ssd_norm)


UP_TN = 256


def _up_ffn_kernel(*refs, sample):
    if sample:
        (a_ref, wg_ref, wv_ref, cwg_ref, cwv_ref, bg_ref, bv_ref, tg_ref, tv_ref, sel_ref,
         act_ref, bufg_ref, bufv_ref) = refs
    else:
        a_ref, wg_ref, wv_ref, cwg_ref, cwv_ref, bg_ref, bv_ref, act_ref, bufg_ref, bufv_ref = refs
        tg_ref = tv_ref = None
    a = a_ref[...]
    tm = a.shape[0]
    t = lax.broadcasted_iota(jnp.int32, (tm, 1), 0)
    if sample:
        t = jnp.bitwise_and(t, DEC_SEQ - 1)

    def conv(u, cw_ref, b_ref, tail_ref):
        out = cw_ref[2:3, :] * u + b_ref[...]
        for s in (1, 2):
            if sample:
                prev = pltpu.roll(tail_ref[...], tm - DEC_SEQ + s, 0)
            else:
                prev = 0.0
            out = out + cw_ref[2 - s:3 - s, :] * jnp.where(t >= s, pltpu.roll(u, s, 0), prev)
        return out

    g = _dot(a, wg_ref[...])
    v = _dot(a, wv_ref[...])
    act_ref[...] = (_silu(conv(g, cwg_ref, bg_ref, tg_ref)) * conv(v, cwv_ref, bv_ref, tv_ref)).astype(bf16)
    if sample:
        sel = sel_ref[...]
        bufg_ref[...] = _dot3r(sel, g)
        bufv_ref[...] = _dot3r(sel, v)
    else:
        bufg_ref[0] = g[tm - 2:tm, :]
        bufv_ref[0] = v[tm - 2:tm, :]


def _up_ffn(a, w_up, w_ffn_conv, b_ffn_conv, tm, tail=None, sel=None):
    m, k = a.shape
    tn = UP_TN
    nj = D_FF // tn
    sample = tail is not None
    wspec = lambda off: pl.BlockSpec((k, tn), lambda i, j: (0, j + off))
    cspec = lambda r, off: pl.BlockSpec((r, tn), lambda i, j: (0, j + off))
    in_specs = [pl.BlockSpec((tm, k), lambda i, j: (i, 0), pipeline_mode=pl.Buffered(1)),
                wspec(0), wspec(nj), cspec(3, 0), cspec(3, nj), cspec(1, 0), cspec(1, nj)]
    args = [a, w_up, w_up, w_ffn_conv, w_ffn_conv, b_ffn_conv, b_ffn_conv]
    if sample:
        in_specs += [pl.BlockSpec((tm, tn), lambda i, j: (i, j)), pl.BlockSpec((tm, tn), lambda i, j: (i, j + nj)),
                     pl.BlockSpec((tm // 2, tm), lambda i, j: (0, 0))]
        args += [tail, tail, sel]
        buf_spec = pl.BlockSpec((tm // 2, tn), lambda i, j: (i, j))
        buf_shape = jax.ShapeDtypeStruct((m // 2, D_FF), f32)
    else:
        buf_spec = pl.BlockSpec((1, 2, tn), lambda i, j: (i, 0, j))
        buf_shape = jax.ShapeDtypeStruct((m // tm, 2, D_FF), f32)
    return pl.pallas_call(
        functools.partial(_up_ffn_kernel, sample=sample),
        grid=(m // tm, nj),
        in_specs=in_specs,
        out_specs=[pl.BlockSpec((tm, tn), lambda i, j: (i, j)), buf_spec, buf_spec],
        out_shape=[jax.ShapeDtypeStruct((m, D_FF), bf16), buf_shape, buf_shape],
        compiler_params=_params("arbitrary", "arbitrary"),
        name="up_ffn_s" if sample else "up_ffn_p",
    )(*args)


def _tile(rows, pref):
    t = min(rows, pref)
    assert rows % t == 0, (rows, pref)
    return t


def _pad_tail(buf):
    b, k1, c = buf.shape
    return jnp.pad(buf, ((0, 0), (DEC_SEQ - k1, 0), (0, 0))).reshape(b * DEC_SEQ, c)


def kernel(x_prompt, x_sample, c_prompt, c_sample, state_sc_conv, state_ssd_conv, state_ssm, state_ffn_conv, w_ada, b_ada, g_norm1, w_in, w_sc_conv, w_ssd_conv, b_ssd_conv, dt_bias, a_log, d_skip, g_ssd_norm, w_out, g_norm2, w_up, w_ffn_conv, b_ffn_conv, w_down, g_final):
    bp, seq, d = x_prompt.shape
    bs, ls, _ = x_sample.shape
    assert d == D_MODEL and ls == DEC_SEQ and bp <= 8 and seq % CHUNK == 0 and (bs * ls) % S1_ROWS == 0
    assert bs % S2_SEQS == 0 and seq <= 2048, "the up-projection row tile holds one whole prompt sequence"
    assert w_ada.shape[0] == 1, "single-layer trunk"
    rp, rs = bp * seq, bs * ls

    xp = x_prompt.reshape(rp, d)
    xs = x_sample.reshape(rs, d)

    c_all = jnp.concatenate([c_prompt, jnp.zeros((8 - bp, d), f32), c_sample], axis=0)
    mod = _ada(c_all, w_ada[0], b_ada[0].reshape(1, -1))
    mod_p = [mod[k, :bp].reshape(bp, 1, d) for k in range(N_MOD)]
    mod_s = [jnp.repeat(mod[k, 8:], ls, axis=0) for k in range(N_MOD)]

    row2 = lambda v: v.reshape(1, -1)
    g1, g2, gf, gssd = row2(g_norm1[0]), row2(g_norm2[0]), row2(g_final), row2(g_ssd_norm[0])
    dtb, alog = row2(dt_bias[0]), row2(a_log[0])
    dexp_flat = jnp.repeat(d_skip[0], HEADDIM).reshape(1, D_SSD)
    dexp_grp = dexp_flat.reshape(N_GROUPS, 1, GROUP_W)
    bssd = row2(b_ssd_conv[0])
    bffn = row2(b_ffn_conv[0])
    gmat = (jnp.arange(D_BC)[:, None] // D_STATE == jnp.arange(N_HEADS)[None, :] // HEADS_PER_GROUP).astype(bf16)
    emat = (jnp.arange(N_HEADS)[:, None] == jnp.arange(D_SSD)[None, :] // HEADDIM).astype(bf16)

    tm_p = _tile(seq, 2048)
    tm_p2 = _tile(seq, 1024)
    tn_norm = _tile(seq, 512)
    w_in_t = jnp.swapaxes(w_in[0], 0, 1)

    h = _normmod(xp, g1, mod_p[1], mod_p[0], tn_norm, seq)
    proj = _matmul_nt(h, w_in_t, tm_p, 512, "in_proj_p")
    mix, p_sc, p_ssd, p_ssm = _mixer_prompt(proj, bp, seq, w_sc_conv[0], w_ssd_conv[0], bssd, dtb, alog,
                                            dexp_grp, gssd)
    x1 = _matmul_res(mix, w_out[0], xp, mod_p[2], tm_p2, 512, seq, "out_proj_p")
    h2 = _normmod(x1, g2, mod_p[4], mod_p[3], tn_norm, seq)
    act, p_fg, p_fv = _up_ffn(h2, w_up[0], w_ffn_conv[0], bffn, seq)
    x2 = _matmul_res(act, w_down[0], x1, mod_p[5], tm_p2, 256, seq, "down_proj_p")
    y_prompt = _final_norm(x2, gf, tn_norm).reshape(bp, seq, d)
    p_ffn = jnp.concatenate([p_fg, p_fv], axis=-1)

    hs = _normmod(xs, g1, mod_s[1], mod_s[0], rs, rs)
    proj_s = _matmul_nt(hs, w_in_t, rs, 512, "in_proj_s")
    ysc, u_s, ypart, xw, eacx, bm, cm, cd = _mixer_s1(
        proj_s, _pad_tail(state_sc_conv[0]), _pad_tail(state_ssd_conv[0]), w_sc_conv[0], w_ssd_conv[0], bssd,
        dtb, alog, dexp_flat, gmat, emat)
    mix_s, s_ssm = _mixer_s2(state_ssm[0], proj_s, ypart, xw, eacx, bm, cm, cd, ysc, gssd)
    x1s = _matmul_res(mix_s, w_out[0], xs, mod_s[2], rs, 512, rs, "out_proj_s")
    h2s = _normmod(x1s, g2, mod_s[4], mod_s[3], rs, rs)
    k_idx = jnp.arange(rs // 2)
    sel = (jnp.arange(rs)[None, :] == ((k_idx // 2) * DEC_SEQ + 2 + k_idx % 2)[:, None]).astype(bf16)
    act_s, s_fg, s_fv = _up_ffn(h2s, w_up[0], w_ffn_conv[0], bffn, rs, tail=_pad_tail(state_ffn_conv[0]), sel=sel)
    x2s = _matmul_res(act_s, w_down[0], x1s, mod_s[5], rs, 256, rs, "down_proj_s")
    y_sample = _final_norm(x2s, gf, rs).reshape(bs, ls, d)

    r3 = lambda v: v.reshape(bs, ls, v.shape[-1])
    s_sc = r3(u_s)[:, ls - 2:]
    s_ssd = r3(proj_s)[:, ls - 3:, OFF_XBC:OFF_XBC + D_XBC]
    s_ffn = jnp.concatenate([s_fg, s_fv], axis=-1).reshape(bs, 2, 2 * D_FF)

    lead = lambda v: v[None]
    return (y_prompt, y_sample, lead(p_sc), lead(p_ssd), lead(p_ssm), lead(p_ffn),
            lead(s_sc), lead(s_ssd), lead(s_ssm), lead(s_ffn))
```

```python
import functools

import jax
import jax.numpy as jnp
from jax import lax
from jax.experimental import pallas as pl
from jax.experimental.pallas import tpu as pltpu

EPS = 1e-6
D_MODEL = 4096
D_SC = 2048
D_SSD = 4096
N_HEADS = 64
HEADDIM = 64
N_GROUPS = 8
HEADS_PER_GROUP = N_HEADS // N_GROUPS
D_STATE = 128
GROUP_W = HEADS_PER_GROUP * HEADDIM
D_BC = N_GROUPS * D_STATE
D_XBC = D_SSD + 2 * D_BC
D_MIX = D_SC + D_SSD
D_IN = 3 * D_SC + D_SSD + D_XBC + N_HEADS
D_FF = 11008
N_MOD = 6
CHUNK = 128
DEC_SEQ = 4

OFF_B, OFF_C, OFF_X, OFF_Z, OFF_XBC, OFF_DT = 0, 2048, 4096, 6144, 10240, 16384

VMEM_LIMIT_BYTES = 56 * 1024 * 1024

_NN = (((1,), (0,)), ((), ()))
_NT = (((1,), (1,)), ((), ()))
_TN = (((0,), (0,)), ((), ()))

f32 = jnp.float32
bf16 = jnp.bfloat16


def _dot(a, b, dims=_NN):
    return lax.dot_general(a, b, dims, preferred_element_type=f32)


def _split3(x):
    hi = x.astype(bf16)
    r1 = x - hi.astype(f32)
    mid = r1.astype(bf16)
    lo = (r1 - mid.astype(f32)).astype(bf16)
    return hi, mid, lo


def _dot3(x, m):
    hi, mid, lo = _split3(x)
    return _dot(hi, m) + _dot(mid, m) + _dot(lo, m)


def _dot3r(m, x):
    hi, mid, lo = _split3(x)
    return _dot(m, hi) + _dot(m, mid) + _dot(m, lo)


def _silu(x):
    return x * jax.nn.sigmoid(x)


def _softplus(x):
    return jnp.maximum(x, 0.0) + jnp.log1p(jnp.exp(-jnp.abs(x)))


def _params(*sem):
    return pltpu.CompilerParams(dimension_semantics=sem, vmem_limit_bytes=VMEM_LIMIT_BYTES)


def _ada_kernel(c_ref, w_ref, b_ref, o_ref):
    a = _silu(c_ref[...]).astype(bf16)
    o_ref[0] = _dot(a, w_ref[...]) + b_ref[...]


def _ada(c_all, w_ada, b_ada):
    rc = c_all.shape[0]
    tn = 512
    nb = D_MODEL // tn
    return pl.pallas_call(
        _ada_kernel,
        grid=(N_MOD * nb,),
        in_specs=[
            pl.BlockSpec((rc, D_MODEL), lambda j: (0, 0)),
            pl.BlockSpec((D_MODEL, tn), lambda j: (0, j)),
            pl.BlockSpec((1, tn), lambda j: (0, j)),
        ],
        out_specs=pl.BlockSpec((1, rc, tn), lambda j: (j // nb, 0, j % nb)),
        out_shape=jax.ShapeDtypeStruct((N_MOD, rc, D_MODEL), f32),
        compiler_params=_params("arbitrary"),
        name="ada",
    )(c_all, w_ada, b_ada)


def _mod_spec(mod, tm, tn, rows_per_seq, ndim_grid):
    if mod.ndim == 3:
        tiles_per_seq = rows_per_seq // tm
        if ndim_grid == 1:
            return pl.BlockSpec((None, 1, tn), lambda i: (i // tiles_per_seq, 0, 0))
        return pl.BlockSpec((None, 1, tn), lambda i, j: (i // tiles_per_seq, 0, j))
    if ndim_grid == 1:
        return pl.BlockSpec((tm, tn), lambda i: (i, 0))
    return pl.BlockSpec((tm, tn), lambda i, j: (i, j))


def _normmod_kernel(x_ref, g_ref, sc_ref, sh_ref, o_ref):
    x = x_ref[...]
    r = lax.rsqrt(jnp.mean(x * x, axis=-1, keepdims=True) + EPS)
    h = (x * r) * g_ref[...]
    o_ref[...] = (h * (1.0 + sc_ref[...]) + sh_ref[...]).astype(o_ref.dtype)


def _normmod(x, g, scale, shift, tm, rows_per_seq):
    rows = x.shape[0]
    return pl.pallas_call(
        _normmod_kernel,
        grid=(rows // tm,),
        in_specs=[
            pl.BlockSpec((tm, D_MODEL), lambda i: (i, 0)),
            pl.BlockSpec((1, D_MODEL), lambda i: (0, 0)),
            _mod_spec(scale, tm, D_MODEL, rows_per_seq, 1),
            _mod_spec(shift, tm, D_MODEL, rows_per_seq, 1),
        ],
        out_specs=pl.BlockSpec((tm, D_MODEL), lambda i: (i, 0)),
        out_shape=jax.ShapeDtypeStruct((rows, D_MODEL), bf16),
        compiler_params=_params("arbitrary"),
        name="normmod",
    )(x, g, scale, shift)


def _norm_kernel(x_ref, g_ref, o_ref):
    x = x_ref[...]
    r = lax.rsqrt(jnp.mean(x * x, axis=-1, keepdims=True) + EPS)
    o_ref[...] = (x * r) * g_ref[...]


def _final_norm(x, g, tm):
    rows = x.shape[0]
    return pl.pallas_call(
        _norm_kernel,
        grid=(rows // tm,),
        in_specs=[
            pl.BlockSpec((tm, D_MODEL), lambda i: (i, 0)),
            pl.BlockSpec((1, D_MODEL), lambda i: (0, 0)),
        ],
        out_specs=pl.BlockSpec((tm, D_MODEL), lambda i: (i, 0)),
        out_shape=jax.ShapeDtypeStruct((rows, D_MODEL), f32),
        compiler_params=_params("arbitrary"),
        name="final_norm",
    )(x, g)


def _mm_kernel(a_ref, w_ref, o_ref):
    o_ref[...] = _dot(a_ref[...], w_ref[...])


def _mm_nt_kernel(a_ref, wt_ref, o_ref):
    o_ref[...] = _dot(a_ref[...], wt_ref[...], _NT)


def _mm_res_kernel(a_ref, w_ref, x_ref, g_ref, o_ref):
    o_ref[...] = x_ref[...] + g_ref[...] * _dot(a_ref[...], w_ref[...])


def _matmul(a, w, tm, tn, name):
    m, k = a.shape
    n = w.shape[1]
    return pl.pallas_call(
        _mm_kernel,
        grid=(m // tm, pl.cdiv(n, tn)),
        in_specs=[
            pl.BlockSpec((tm, k), lambda i, j: (i, 0), pipeline_mode=pl.Buffered(1)),
            pl.BlockSpec((k, tn), lambda i, j: (0, j)),
        ],
        out_specs=pl.BlockSpec((tm, tn), lambda i, j: (i, j)),
        out_shape=jax.ShapeDtypeStruct((m, n), f32),
        compiler_params=_params("arbitrary", "arbitrary"),
        name=name,
    )(a, w)


def _matmul_nt(a, wt, tm, tn, name):
    m, k = a.shape
    n = wt.shape[0]
    return pl.pallas_call(
        _mm_nt_kernel,
        grid=(m // tm, pl.cdiv(n, tn)),
        in_specs=[
            pl.BlockSpec((tm, k), lambda i, j: (i, 0), pipeline_mode=pl.Buffered(1)),
            pl.BlockSpec((tn, k), lambda i, j: (j, 0)),
        ],
        out_specs=pl.BlockSpec((tm, tn), lambda i, j: (i, j)),
        out_shape=jax.ShapeDtypeStruct((m, n), f32),
        compiler_params=_params("arbitrary", "arbitrary"),
        name=name,
    )(a, wt)


def _matmul_res(a, w, x, gate, tm, tn, rows_per_seq, name):
    m, k = a.shape
    n = w.shape[1]
    return pl.pallas_call(
        _mm_res_kernel,
        grid=(m // tm, n // tn),
        in_specs=[
            pl.BlockSpec((tm, k), lambda i, j: (i, 0), pipeline_mode=pl.Buffered(1)),
            pl.BlockSpec((k, tn), lambda i, j: (0, j)),
            pl.BlockSpec((tm, tn), lambda i, j: (i, j)),
            _mod_spec(gate, tm, tn, rows_per_seq, 2),
        ],
        out_specs=pl.BlockSpec((tm, tn), lambda i, j: (i, j)),
        out_shape=jax.ShapeDtypeStruct((m, n), f32),
        compiler_params=_params("arbitrary", "arbitrary"),
        name=name,
    )(a, w, x, gate)


def _mixer_prompt_kernel(proj_ref, wsc_ref, wssd_ref, bssd_ref, dtb_ref, alog_ref, dexp_ref, gssd_ref, emat_ref,
                         mix_ref, scbuf_ref, ssdbuf_ref, state_ref,
                         uwin, xwin, xs_s, b_s, c_s, y_s, ac_s, dt_s, w_s, eac_s, act_s, cd_s):
    q = CHUNK
    c = pl.program_id(1)

    @pl.when(c == 0)
    def _init():
        uwin[0:8, :] = jnp.zeros((8, D_SC), f32)
        xwin[0:8, :] = jnp.zeros((8, D_XBC), f32)
        state_ref[...] = jnp.zeros(state_ref.shape, f32)

    u = proj_ref[:, OFF_C:OFF_C + D_SC] * proj_ref[:, OFF_X:OFF_X + D_SC]
    uwin[8:8 + q, :] = u
    uc = (wsc_ref[2:3, :] * u + wsc_ref[1:2, :] * uwin[7:7 + q, :] + wsc_ref[0:1, :] * uwin[6:6 + q, :])
    mix_ref[:, 0:D_SC] = (proj_ref[:, OFF_B:OFF_B + D_SC] * uc).astype(bf16)
    scbuf_ref[0] = uwin[q + 6:q + 8, :]
    uwin[0:8, :] = uwin[q:q + 8, :]

    xwin[8:8 + q, :] = proj_ref[:, OFF_XBC:OFF_XBC + D_XBC]

    def conv_slab(lo, width):
        acc = (wssd_ref[3:4, lo:lo + width] * xwin[8:8 + q, lo:lo + width]
               + wssd_ref[2:3, lo:lo + width] * xwin[7:7 + q, lo:lo + width]
               + wssd_ref[1:2, lo:lo + width] * xwin[6:6 + q, lo:lo + width]
               + wssd_ref[0:1, lo:lo + width] * xwin[5:5 + q, lo:lo + width]
               + bssd_ref[:, lo:lo + width])
        return _silu(acc)

    for g in range(N_GROUPS):
        xs_s[g] = conv_slab(g * GROUP_W, GROUP_W)
        b_s[g] = conv_slab(D_SSD + g * D_STATE, D_STATE)
        c_s[g] = conv_slab(D_SSD + D_BC + g * D_STATE, D_STATE)
    ssdbuf_ref[0] = xwin[q + 5:q + 8, :]
    xwin[0:8, :] = xwin[q:q + 8, :]

    dt = _softplus(proj_ref[:, OFF_DT:OFF_DT + N_HEADS] + dtb_ref[...])
    a = dt * (-jnp.exp(alog_ref[...]))
    row = lax.broadcasted_iota(jnp.int32, (q, q), 0)
    col = lax.broadcasted_iota(jnp.int32, (q, q), 1)
    tri = col <= row
    ac = _dot3r(jnp.where(tri, 1.0, 0.0).astype(bf16), a)
    ac_last = ac[q - 1:q, :]
    wv = dt * jnp.exp(ac_last - ac)
    eac = jnp.exp(ac)
    cd = jnp.exp(ac_last)
    ac_t = ac.T
    for g in range(N_GROUPS):
        hs = slice(g * HEADS_PER_GROUP, (g + 1) * HEADS_PER_GROUP)
        ac_s[g] = ac[:, hs]
        em = emat_ref[:, g * GROUP_W:(g + 1) * GROUP_W]
        dt_s[g] = _dot3(dt, em)
        w_s[g] = _dot3(wv, em)
        eac_s[g] = _dot3(eac, em)
        cd_s[g] = cd[:, hs]
        act_s[g] = ac_t[hs, :]

    lane_lo = lax.broadcasted_iota(jnp.int32, (q, 2 * HEADDIM), 1) < HEADDIM

    def bcast_col(arr, r):
        return jnp.broadcast_to(arr[:, r:r + 1], (q, 2 * HEADDIM))

    def group_body(g, carry):
        bb = b_s[g].astype(bf16)
        cb_ = c_s[g].astype(bf16)
        cb = _dot(cb_, bb, _NT)
        h0 = pl.multiple_of(g * HEADS_PER_GROUP, HEADS_PER_GROUP)
        s_old = state_ref[0, pl.ds(h0, HEADS_PER_GROUP)].reshape(GROUP_W, D_STATE)
        yoff = _dot(cb_, s_old.astype(bf16), _NT)
        xg = xs_s[g]
        acg, dtg, wg, eg, atg, cdg = ac_s[g], dt_s[g], w_s[g], eac_s[g], act_s[g], cd_s[g]
        dsk = dexp_ref[g]
        xw_parts = []
        for pair in range(HEADS_PER_GROUP // 2):
            cs = slice(pair * 2 * HEADDIM, (pair + 1) * 2 * HEADDIM)
            r0, r1 = 2 * pair, 2 * pair + 1
            xp = xg[:, cs]
            xdt = xp * dtg[:, cs]
            ydiag = jnp.zeros((q, 2 * HEADDIM), f32)
            for r, keep in ((r0, lane_lo), (r1, jnp.logical_not(lane_lo))):
                seg = bcast_col(acg, r) - jnp.broadcast_to(atg[r:r + 1, :], (q, q))
                decay = jnp.where(tri, jnp.exp(jnp.where(tri, seg, 0.0)), 0.0)
                m = (cb * decay).astype(bf16)
                ydiag = ydiag + _dot(m, jnp.where(keep, xdt, 0.0).astype(bf16))
            y_s[g, :, cs] = ydiag + yoff[:, cs] * eg[:, cs] + dsk[:, cs] * xp
            xw_parts.append(xp * wg[:, cs])
        xw = jnp.concatenate(xw_parts, axis=1).astype(bf16)
        upd = _dot(xw, bb, _TN)
        for r in range(HEADS_PER_GROUP):
            rs = slice(r * HEADDIM, (r + 1) * HEADDIM)
            cdr = jnp.broadcast_to(cdg[:, r:r + 1], (HEADDIM, D_STATE))
            state_ref[0, h0 + r] = s_old[rs] * cdr + upd[rs]
        return carry

    lax.fori_loop(0, N_GROUPS, group_body, 0)

    ss = jnp.zeros((q, 1), f32)
    for g in range(N_GROUPS):
        z = proj_ref[:, OFF_Z + g * GROUP_W:OFF_Z + (g + 1) * GROUP_W]
        v = y_s[g] * _silu(z)
        y_s[g] = v
        ss = ss + jnp.sum(v * v, axis=-1, keepdims=True)
    rn = lax.rsqrt(ss / D_SSD + EPS)
    for g in range(N_GROUPS):
        cs = slice(g * GROUP_W, (g + 1) * GROUP_W)
        mix_ref[:, D_SC + g * GROUP_W:D_SC + (g + 1) * GROUP_W] = ((y_s[g] * rn) * gssd_ref[:, cs]).astype(bf16)


def _mixer_prompt(proj, bp, seq, w_sc_conv, w_ssd_conv, b_ssd_conv, dt_bias, a_log, dexp, g_ssd_norm, emat):
    nc = seq // CHUNK
    rows = bp * seq
    full = lambda shape: pl.BlockSpec(shape, lambda b, c: (0,) * len(shape))
    hp = HEADS_PER_GROUP
    return pl.pallas_call(
        _mixer_prompt_kernel,
        grid=(bp, nc),
        in_specs=[
            pl.BlockSpec((CHUNK, D_IN), lambda b, c: (b * nc + c, 0)),
            full((3, D_SC)), full((4, D_XBC)), full((1, D_XBC)), full((1, N_HEADS)), full((1, N_HEADS)),
            full((N_GROUPS, 1, GROUP_W)), full((1, D_SSD)), full((N_HEADS, D_SSD)),
        ],
        out_specs=[
            pl.BlockSpec((CHUNK, D_MIX), lambda b, c: (b * nc + c, 0)),
            pl.BlockSpec((1, 2, D_SC), lambda b, c: (b, 0, 0)),
            pl.BlockSpec((1, 3, D_XBC), lambda b, c: (b, 0, 0)),
            pl.BlockSpec((1, N_HEADS, HEADDIM, D_STATE), lambda b, c: (b, 0, 0, 0)),
        ],
        out_shape=[
            jax.ShapeDtypeStruct((rows, D_MIX), bf16),
            jax.ShapeDtypeStruct((bp, 2, D_SC), f32),
            jax.ShapeDtypeStruct((bp, 3, D_XBC), f32),
            jax.ShapeDtypeStruct((bp, N_HEADS, HEADDIM, D_STATE), f32),
        ],
        scratch_shapes=[
            pltpu.VMEM((8 + CHUNK, D_SC), f32),
            pltpu.VMEM((8 + CHUNK, D_XBC), f32),
            pltpu.VMEM((N_GROUPS, CHUNK, GROUP_W), f32),
            pltpu.VMEM((N_GROUPS, CHUNK, D_STATE), f32),
            pltpu.VMEM((N_GROUPS, CHUNK, D_STATE), f32),
            pltpu.VMEM((N_GROUPS, CHUNK, GROUP_W), f32),
            pltpu.VMEM((N_GROUPS, CHUNK, hp), f32),
            pltpu.VMEM((N_GROUPS, CHUNK, GROUP_W), f32),
            pltpu.VMEM((N_GROUPS, CHUNK, GROUP_W), f32),
            pltpu.VMEM((N_GROUPS, CHUNK, GROUP_W), f32),
            pltpu.VMEM((N_GROUPS, hp, CHUNK), f32),
            pltpu.VMEM((N_GROUPS, 1, hp), f32),
        ],
        compiler_params=_params("arbitrary", "arbitrary"),
        name="mixer_prompt",
    )(proj, w_sc_conv, w_ssd_conv, b_ssd_conv, dt_bias, a_log, dexp, g_ssd_norm, emat)


S1_ROWS = 128


def _mixer_s1_kernel(proj_ref, scp_ref, ssdp_ref, wsc_ref, wssd_ref, bssd_ref, dtb_ref, alog_ref, dexp_ref,
                     gmat_ref, emat_ref,
                     ysc_ref, u_ref, ypart_ref, xw_ref, eacx_ref, b_ref, c_ref, cd_ref):
    n = S1_ROWS
    t = jnp.bitwise_and(lax.broadcasted_iota(jnp.int32, (n, 1), 0), DEC_SEQ - 1)

    def shifted(x, tail, s):
        return jnp.where(t >= s, pltpu.roll(x, s, 0), pltpu.roll(tail, n - DEC_SEQ + s, 0))

    def shifted0(x, s):
        return jnp.where(t >= s, pltpu.roll(x, s, 0), 0.0)

    u = proj_ref[:, OFF_C:OFF_C + D_SC] * proj_ref[:, OFF_X:OFF_X + D_SC]
    scp = scp_ref[...]
    uc = wsc_ref[2:3, :] * u + wsc_ref[1:2, :] * shifted(u, scp, 1) + wsc_ref[0:1, :] * shifted(u, scp, 2)
    ysc_ref[...] = (proj_ref[:, OFF_B:OFF_B + D_SC] * uc).astype(bf16)
    u_ref[...] = u

    xbc = proj_ref[:, OFF_XBC:OFF_XBC + D_XBC]
    tail = ssdp_ref[...]
    acc = (wssd_ref[3:4, :] * xbc + wssd_ref[2:3, :] * shifted(xbc, tail, 1)
           + wssd_ref[1:2, :] * shifted(xbc, tail, 2) + wssd_ref[0:1, :] * shifted(xbc, tail, 3) + bssd_ref[...])
    xc = _silu(acc)
    xs = xc[:, 0:D_SSD]
    bm = xc[:, D_SSD:D_SSD + D_BC]
    cm = xc[:, D_SSD + D_BC:D_XBC]
    b_ref[...] = bm
    c_ref[...] = cm

    dt = _softplus(proj_ref[:, OFF_DT:OFF_DT + N_HEADS] + dtb_ref[...])
    a = dt * (-jnp.exp(alog_ref[...]))
    ac = a + shifted0(a, 1) + shifted0(a, 2) + shifted0(a, 3)
    ac_last = jnp.where(t == DEC_SEQ - 1, ac, 0.0)
    for d in range(1, DEC_SEQ):
        ac_last = ac_last + jnp.where(t == DEC_SEQ - 1 - d, pltpu.roll(ac, n - d, 0), 0.0)
    wv = dt * jnp.exp(ac_last - ac)
    eac = jnp.exp(ac)
    cd_ref[...] = jnp.exp(ac_last)

    gmat = gmat_ref[...]
    emat = emat_ref[...]
    y = dexp_ref[...] * xs
    for s in range(DEC_SEQ):
        if s == 0:
            coef = _dot3(cm * bm, gmat) * dt
            x_s = xs
        else:
            live = t >= s
            cbh = _dot3(cm * pltpu.roll(bm, s, 0), gmat)
            dec = jnp.exp(jnp.where(live, ac - pltpu.roll(ac, s, 0), 0.0))
            coef = jnp.where(live, cbh * dec * pltpu.roll(dt, s, 0), 0.0)
            x_s = pltpu.roll(xs, s, 0)
        y = y + _dot3(coef, emat) * x_s
    ypart_ref[...] = y
    xw_ref[...] = xs * _dot3(wv, emat)
    eacx_ref[...] = _dot3(eac, emat)


def _mixer_s1(proj, scp, ssdp, w_sc_conv, w_ssd_conv, b_ssd_conv, dt_bias, a_log, dexp, gmat, emat):
    rows = proj.shape[0]
    n = S1_ROWS
    full = lambda shape: pl.BlockSpec(shape, lambda i: (0,) * len(shape))
    rowblk = lambda w: pl.BlockSpec((n, w), lambda i: (i, 0))
    return pl.pallas_call(
        _mixer_s1_kernel,
        grid=(rows // n,),
        in_specs=[
            rowblk(D_IN), rowblk(D_SC), rowblk(D_XBC),
            full((3, D_SC)), full((4, D_XBC)), full((1, D_XBC)), full((1, N_HEADS)), full((1, N_HEADS)),
            full((1, D_SSD)), full((D_BC, N_HEADS)), full((N_HEADS, D_SSD)),
        ],
        out_specs=[rowblk(D_SC), rowblk(D_SC), rowblk(D_SSD), rowblk(D_SSD), rowblk(D_SSD),
                   rowblk(D_BC), rowblk(D_BC), rowblk(N_HEADS)],
        out_shape=[
            jax.ShapeDtypeStruct((rows, D_SC), bf16),
            jax.ShapeDtypeStruct((rows, D_SC), f32),
            jax.ShapeDtypeStruct((rows, D_SSD), f32),
            jax.ShapeDtypeStruct((rows, D_SSD), f32),
            jax.ShapeDtypeStruct((rows, D_SSD), f32),
            jax.ShapeDtypeStruct((rows, D_BC), f32),
            jax.ShapeDtypeStruct((rows, D_BC), f32),
            jax.ShapeDtypeStruct((rows, N_HEADS), f32),
        ],
        compiler_params=_params("arbitrary"),
        name="mixer_s1",
    )(proj, scp, ssdp, w_sc_conv, w_ssd_conv, b_ssd_conv, dt_bias, a_log, dexp, gmat, emat)


S2_SEQS = 4
S2_ROWS = S2_SEQS * DEC_SEQ


def _mixer_s2_kernel(state_ref, proj_ref, ypart_ref, xw_ref, eacx_ref, b_ref, c_ref, cd_ref, ysc_ref, gssd_ref,
                     mix_ref, snew_ref):
    mix_ref[:, 0:D_SC] = ysc_ref[...]
    for e in range(S2_SEQS):
        tr = slice(e * DEC_SEQ, (e + 1) * DEC_SEQ)
        vs = []
        ss = jnp.zeros((DEC_SEQ, 1), f32)
        for g in range(N_GROUPS):
            cs = slice(g * GROUP_W, (g + 1) * GROUP_W)
            ns = slice(g * D_STATE, (g + 1) * D_STATE)
            h0 = g * HEADS_PER_GROUP
            s_old = state_ref[e, h0:h0 + HEADS_PER_GROUP].reshape(GROUP_W, D_STATE)
            cg = c_ref[tr, ns].astype(bf16)
            bg = b_ref[tr, ns].astype(bf16)
            yoff = _dot(cg, s_old.astype(bf16), _NT)
            upd = _dot(xw_ref[tr, cs].astype(bf16), bg, _TN)
            for r in range(HEADS_PER_GROUP):
                rs = slice(r * HEADDIM, (r + 1) * HEADDIM)
                row0 = e * DEC_SEQ
                cdr = jnp.broadcast_to(cd_ref[row0:row0 + 1, h0 + r:h0 + r + 1], (HEADDIM, D_STATE))
                snew_ref[e, h0 + r] = s_old[rs] * cdr + upd[rs]
            yg = ypart_ref[tr, cs] + yoff * eacx_ref[tr, cs]
            v = yg * _silu(proj_ref[tr, OFF_Z + g * GROUP_W:OFF_Z + (g + 1) * GROUP_W])
            vs.append(v)
            ss = ss + jnp.sum(v * v, axis=-1, keepdims=True)
        rn = lax.rsqrt(ss / D_SSD + EPS)
        for g in range(N_GROUPS):
            cs = slice(g * GROUP_W, (g + 1) * GROUP_W)
            mix_ref[tr, D_SC + g * GROUP_W:D_SC + (g + 1) * GROUP_W] = (
                (vs[g] * rn) * gssd_ref[:, cs]).astype(bf16)


def _mixer_s2(state, proj, ypart, xw, eacx, bm, cm, cd, ysc, g_ssd_norm):
    bs = state.shape[0]
    rowblk = lambda w: pl.BlockSpec((S2_ROWS, w), lambda i: (i, 0))
    stblk = pl.BlockSpec((S2_SEQS, N_HEADS, HEADDIM, D_STATE), lambda i: (i, 0, 0, 0))
    return pl.pallas_call(
        _mixer_s2_kernel,
        grid=(bs // S2_SEQS,),
        in_specs=[stblk, rowblk(D_IN), rowblk(D_SSD), rowblk(D_SSD), rowblk(D_SSD), rowblk(D_BC), rowblk(D_BC),
                  rowblk(N_HEADS), rowblk(D_SC), pl.BlockSpec((1, D_SSD), lambda i: (0, 0))],
        out_specs=[rowblk(D_MIX), stblk],
        out_shape=[
            jax.ShapeDtypeStruct((bs * DEC_SEQ, D_MIX), bf16),
            jax.ShapeDtypeStruct((bs, N_HEADS, HEADDIM, D_STATE), f32),
        ],
        compiler_params=_params("arbitrary"),
        name="mixer_s2",
    )(state, proj, ypart, xw, eacx, bm, cm, cd, ysc, g_ssd_norm)


UP_TN = 256


def _up_ffn_kernel(*refs, sample):
    if sample:
        (a_ref, wg_ref, wv_ref, cwg_ref, cwv_ref, bg_ref, bv_ref, tg_ref, tv_ref, sel_ref,
         act_ref, bufg_ref, bufv_ref) = refs
    else:
        a_ref, wg_ref, wv_ref, cwg_ref, cwv_ref, bg_ref, bv_ref, act_ref, bufg_ref, bufv_ref = refs
        tg_ref = tv_ref = None
    a = a_ref[...]
    tm = a.shape[0]
    t = lax.broadcasted_iota(jnp.int32, (tm, 1), 0)
    if sample:
        t = jnp.bitwise_and(t, DEC_SEQ - 1)

    def conv(u, cw_ref, b_ref, tail_ref):
        out = cw_ref[2:3, :] * u + b_ref[...]
        for s in (1, 2):
            if sample:
                prev = pltpu.roll(tail_ref[...], tm - DEC_SEQ + s, 0)
            else:
                prev = 0.0
            out = out + cw_ref[2 - s:3 - s, :] * jnp.where(t >= s, pltpu.roll(u, s, 0), prev)
        return out

    g = _dot(a, wg_ref[...])
    v = _dot(a, wv_ref[...])
    act_ref[...] = (_silu(conv(g, cwg_ref, bg_ref, tg_ref)) * conv(v, cwv_ref, bv_ref, tv_ref)).astype(bf16)
    if sample:
        sel = sel_ref[...]
        bufg_ref[...] = _dot3r(sel, g)
        bufv_ref[...] = _dot3r(sel, v)
    else:
        bufg_ref[0] = g[tm - 2:tm, :]
        bufv_ref[0] = v[tm - 2:tm, :]


def _up_ffn(a, w_up, w_ffn_conv, b_ffn_conv, tm, tail=None, sel=None):
    m, k = a.shape
    tn = UP_TN
    nj = D_FF // tn
    sample = tail is not None
    wspec = lambda off: pl.BlockSpec((k, tn), lambda i, j: (0, j + off))
    cspec = lambda r, off: pl.BlockSpec((r, tn), lambda i, j: (0, j + off))
    in_specs = [pl.BlockSpec((tm, k), lambda i, j: (i, 0), pipeline_mode=pl.Buffered(1)),
                wspec(0), wspec(nj), cspec(3, 0), cspec(3, nj), cspec(1, 0), cspec(1, nj)]
    args = [a, w_up, w_up, w_ffn_conv, w_ffn_conv, b_ffn_conv, b_ffn_conv]
    if sample:
        in_specs += [pl.BlockSpec((tm, tn), lambda i, j: (i, j)), pl.BlockSpec((tm, tn), lambda i, j: (i, j + nj)),
                     pl.BlockSpec((tm // 2, tm), lambda i, j: (0, 0))]
        args += [tail, tail, sel]
        buf_spec = pl.BlockSpec((tm // 2, tn), lambda i, j: (i, j))
        buf_shape = jax.ShapeDtypeStruct((m // 2, D_FF), f32)
    else:
        buf_spec = pl.BlockSpec((1, 2, tn), lambda i, j: (i, 0, j))
        buf_shape = jax.ShapeDtypeStruct((m // tm, 2, D_FF), f32)
    return pl.pallas_call(
        functools.partial(_up_ffn_kernel, sample=sample),
        grid=(m // tm, nj),
        in_specs=in_specs,
        out_specs=[pl.BlockSpec((tm, tn), lambda i, j: (i, j)), buf_spec, buf_spec],
        out_shape=[jax.ShapeDtypeStruct((m, D_FF), bf16), buf_shape, buf_shape],
        compiler_params=_params("arbitrary", "arbitrary"),
        name="up_ffn_s" if sample else "up_ffn_p",
    )(*args)


def _tile(rows, pref):
    t = min(rows, pref)
    assert rows % t == 0, (rows, pref)
    return t


def _pad_tail(buf):
    b, k1, c = buf.shape
    return jnp.pad(buf, ((0, 0), (DEC_SEQ - k1, 0), (0, 0))).reshape(b * DEC_SEQ, c)


def kernel(x_prompt, x_sample, c_prompt, c_sample, state_sc_conv, state_ssd_conv, state_ssm, state_ffn_conv, w_ada, b_ada, g_norm1, w_in, w_sc_conv, w_ssd_conv, b_ssd_conv, dt_bias, a_log, d_skip, g_ssd_norm, w_out, g_norm2, w_up, w_ffn_conv, b_ffn_conv, w_down, g_final):
    bp, seq, d = x_prompt.shape
    bs, ls, _ = x_sample.shape
    assert d == D_MODEL and ls == DEC_SEQ and bp <= 8 and seq % CHUNK == 0 and (bs * ls) % S1_ROWS == 0
    assert bs % S2_SEQS == 0 and seq <= 2048, "the up-projection row tile holds one whole prompt sequence"
    assert w_ada.shape[0] == 1, "single-layer trunk"
    rp, rs = bp * seq, bs * ls

    xp = x_prompt.reshape(rp, d)
    xs = x_sample.reshape(rs, d)

    c_all = jnp.concatenate([c_prompt, jnp.zeros((8 - bp, d), f32), c_sample], axis=0)
    mod = _ada(c_all, w_ada[0], b_ada[0].reshape(1, -1))
    mod_p = [mod[k, :bp].reshape(bp, 1, d) for k in range(N_MOD)]
    mod_s = [jnp.repeat(mod[k, 8:], ls, axis=0) for k in range(N_MOD)]

    row2 = lambda v: v.reshape(1, -1)
    g1, g2, gf, gssd = row2(g_norm1[0]), row2(g_norm2[0]), row2(g_final), row2(g_ssd_norm[0])
    dtb, alog = row2(dt_bias[0]), row2(a_log[0])
    dexp_flat = jnp.repeat(d_skip[0], HEADDIM).reshape(1, D_SSD)
    dexp_grp = dexp_flat.reshape(N_GROUPS, 1, GROUP_W)
    bssd = row2(b_ssd_conv[0])
    bffn = row2(b_ffn_conv[0])
    gmat = (jnp.arange(D_BC)[:, None] // D_STATE == jnp.arange(N_HEADS)[None, :] // HEADS_PER_GROUP).astype(bf16)
    emat = (jnp.arange(N_HEADS)[:, None] == jnp.arange(D_SSD)[None, :] // HEADDIM).astype(bf16)

    tm_p = _tile(seq, 2048)
    tm_p2 = _tile(seq, 1024)
    tn_norm = _tile(seq, 512)
    w_in_t = jnp.swapaxes(w_in[0], 0, 1)

    h = _normmod(xp, g1, mod_p[1], mod_p[0], tn_norm, seq)
    proj = _matmul_nt(h, w_in_t, tm_p, 512, "in_proj_p")
    mix, p_sc, p_ssd, p_ssm = _mixer_prompt(proj, bp, seq, w_sc_conv[0], w_ssd_conv[0], bssd, dtb, alog,
                                            dexp_grp, gssd, emat)
    x1 = _matmul_res(mix, w_out[0], xp, mod_p[2], tm_p2, 512, seq, "out_proj_p")
    h2 = _normmod(x1, g2, mod_p[4], mod_p[3], tn_norm, seq)
    act, p_fg, p_fv = _up_ffn(h2, w_up[0], w_ffn_conv[0], bffn, seq)
    x2 = _matmul_res(act, w_down[0], x1, mod_p[5], tm_p2, 256, seq, "down_proj_p")
    y_prompt = _final_norm(x2, gf, tn_norm).reshape(bp, seq, d)
    p_ffn = jnp.concatenate([p_fg, p_fv], axis=-1)

    hs = _normmod(xs, g1, mod_s[1], mod_s[0], rs, rs)
    proj_s = _matmul_nt(hs, w_in_t, rs, 512, "in_proj_s")
    ysc, u_s, ypart, xw, eacx, bm, cm, cd = _mixer_s1(
        proj_s, _pad_tail(state_sc_conv[0]), _pad_tail(state_ssd_conv[0]), w_sc_conv[0], w_ssd_conv[0], bssd,
        dtb, alog, dexp_flat, gmat, emat)
    mix_s, s_ssm = _mixer_s2(state_ssm[0], proj_s, ypart, xw, eacx, bm, cm, cd, ysc, gssd)
    x1s = _matmul_res(mix_s, w_out[0], xs, mod_s[2], rs, 512, rs, "out_proj_s")
    h2s = _normmod(x1s, g2, mod_s[4], mod_s[3], rs, rs)
    k_idx = jnp.arange(rs // 2)
    sel = (jnp.arange(rs)[None, :] == ((k_idx // 2) * DEC_SEQ + 2 + k_idx % 2)[:, None]).astype(bf16)
    act_s, s_fg, s_fv = _up_ffn(h2s, w_up[0], w_ffn_conv[0], bffn, rs, tail=_pad_tail(state_ffn_conv[0]), sel=sel)
    x2s = _matmul_res(act_s, w_down[0], x1s, mod_s[5], rs, 256, rs, "down_proj_s")
    y_sample = _final_norm(x2s, gf, rs).reshape(bs, ls, d)

    r3 = lambda v: v.reshape(bs, ls, v.shape[-1])
    s_sc = r3(u_s)[:, ls - 2:]
    s_ssd = r3(proj_s)[:, ls - 3:, OFF_XBC:OFF_XBC + D_XBC]
    s_ffn = jnp.concatenate([s_fg, s_fv], axis=-1).reshape(bs, 2, 2 * D_FF)

    lead = lambda v: v[None]
    return (y_prompt, y_sample, lead(p_sc), lead(p_ssd), lead(p_ssm), lead(p_ffn),
            lead(s_sc), lead(s_ssd), lead(s_ssm), lead(s_ffn))
```
